```python
import jax
import jax.numpy as jnp
from jax import lax
import numpy as np

D_MODEL = 2048
BATCH = 4
SEQ = 4096
DEPTH = 1

CONV_WIDTH = D_MODEL // 2
CONV_KERNEL = 31
ATTN_WIDTH = D_MODEL - CONV_WIDTH
N_HEADS = 16
HEAD_DIM = ATTN_WIDTH // N_HEADS
MIX_WIDTH = CONV_WIDTH + ATTN_WIDTH
DILATED_PATTERNS = ((128, 1), (512, 4), (2048, 16))
SPLITS = (CONV_WIDTH, CONV_WIDTH, CONV_WIDTH, ATTN_WIDTH, ATTN_WIDTH, ATTN_WIDTH, ATTN_WIDTH)
IN_WIDTH = sum(SPLITS)
SPLIT_POINTS = tuple(int(i) for i in np.cumsum(SPLITS)[:-1])
RMS_EPS = 1e-6
LN_EPS = 1e-5
MASK_VALUE = -1e30

kernel_name = "hybrid_conformer_conv_dilated_attn"


def rms_norm(x, g):
    xf = x.astype(jnp.float32)
    y = xf * lax.rsqrt(jnp.mean(xf * xf, axis=-1, keepdims=True) + RMS_EPS)
    return (y * g.astype(jnp.float32)).astype(x.dtype)


def layer_norm(x, g, b):
    xf = x.astype(jnp.float32)
    mu = jnp.mean(xf, axis=-1, keepdims=True)
    var = jnp.mean(jnp.square(xf - mu), axis=-1, keepdims=True)
    y = (xf - mu) * lax.rsqrt(var + LN_EPS)
    return (y * g.astype(jnp.float32) + b.astype(jnp.float32)).astype(x.dtype)


def causal_depthwise_conv(u, w, b):
    k, c = w.shape
    y = lax.conv_general_dilated(
        u, w[:, None, :].astype(u.dtype), window_strides=(1,),
        padding=[(k - 1, 0)], dimension_numbers=("NWC", "WIO", "NWC"),
        feature_group_count=c)
    return y + b.astype(u.dtype)


def dilated_window_attention(q, k, v, window, dilation):
    b, s, h, dh = q.shape
    nb = window // dilation
    sub_len = -(-s // dilation)
    n_blk = -(-sub_len // nb)
    s_pad = n_blk * nb * dilation
    pad = ((0, 0), (0, s_pad - s), (0, 0), (0, 0))

    def to_blocks(t):
        return jnp.pad(t, pad).reshape(b, n_blk, nb, dilation, h, dh)

    def with_prev(t):
        prev = jnp.concatenate([jnp.zeros_like(t[:, :1]), t[:, :-1]], axis=1)
        return jnp.concatenate([prev, t], axis=2)

    qb = to_blocks(q)
    kb = with_prev(to_blocks(k))
    vb = with_prev(to_blocks(v))
    scores = jnp.einsum("bnqrhd,bnkrhd->bnrhqk", qb, kb,
                        preferred_element_type=jnp.float32) * (dh ** -0.5)
    qi = jnp.arange(nb)[:, None]
    ki = jnp.arange(2 * nb)[None, :]
    delta = nb + qi - ki
    band = (delta >= 0) & (delta <= nb)
    valid = band[None] & ((jnp.arange(n_blk)[:, None, None] > 0) | (ki[None] >= nb))
    scores = jnp.where(valid[None, :, None, None], scores, MASK_VALUE)
    m = jnp.max(scores, axis=-1, keepdims=True)
    p = jnp.exp(scores - m)
    l = jnp.sum(p, axis=-1)
    o = jnp.einsum("bnrhqk,bnkrhd->bnqrhd", p.astype(v.dtype), vb,
                   preferred_element_type=jnp.float32)
    o = o / jnp.transpose(l, (0, 1, 4, 2, 3))[..., None]
    lse = jnp.transpose(m[..., 0] + jnp.log(l), (0, 1, 4, 2, 3))
    o = o.reshape(b, s_pad, h, dh)[:, :s]
    lse = lse.reshape(b, s_pad, h)[:, :s]
    return o, lse


def dilated_attention_mixture(q, k, v):
    outs, lses = [], []
    for window, dilation in DILATED_PATTERNS:
        o, lse = dilated_window_attention(q, k, v, window, dilation)
        outs.append(o)
        lses.append(lse)
    w = jax.nn.softmax(jnp.stack(lses, axis=0), axis=0)
    return jnp.einsum("pbsh,pbshd->bshd", w, jnp.stack(outs, axis=0))


def hybrid_layer(x, norm_g, w_in, conv_w, conv_b, conv_norm_g, conv_norm_b,
                 conv_pw_w, conv_pw_b, q_norm_g, k_norm_g, w_out):
    bsz, s, _ = x.shape
    h = rms_norm(x, norm_g)
    z = h @ w_in
    a_val, a_glu, a_gate, q, k, v, b_gate = jnp.split(z, SPLIT_POINTS, axis=-1)

    u = a_val * jax.nn.sigmoid(a_glu)
    u = causal_depthwise_conv(u, conv_w, conv_b)
    u = jax.nn.silu(layer_norm(u, conv_norm_g, conv_norm_b))
    u = u @ conv_pw_w + conv_pw_b
    y_a = u * jax.nn.silu(a_gate)

    q = rms_norm(q.reshape(bsz, s, N_HEADS, HEAD_DIM), q_norm_g)
    k = rms_norm(k.reshape(bsz, s, N_HEADS, HEAD_DIM), k_norm_g)
    v = v.reshape(bsz, s, N_HEADS, HEAD_DIM)
    o = dilated_attention_mixture(q, k, v).astype(x.dtype)
    y_b = o.reshape(bsz, s, ATTN_WIDTH) * jax.nn.silu(b_gate)

    y = jnp.concatenate([y_a, y_b], axis=-1) @ w_out
    return x + y


def setup_inputs(seed: int = 0) -> dict:
    key = jax.random.key(seed)
    ks = jax.random.split(key, 13)
    f32 = jnp.float32
    nrm = lambda k, shape, scale: jax.random.normal(k, shape, f32) * scale
    return {
        "x": nrm(ks[0], (BATCH, SEQ, D_MODEL), 1.0),
        "norm_g": 1.0 + nrm(ks[1], (DEPTH, D_MODEL), 0.02),
        "w_in": nrm(ks[2], (DEPTH, D_MODEL, IN_WIDTH), D_MODEL ** -0.5),
        "conv_w": nrm(ks[3], (DEPTH, CONV_KERNEL, CONV_WIDTH), CONV_KERNEL ** -0.5),
        "conv_b": nrm(ks[4], (DEPTH, CONV_WIDTH), 0.01),
        "conv_norm_g": 1.0 + nrm(ks[5], (DEPTH, CONV_WIDTH), 0.02),
        "conv_norm_b": nrm(ks[6], (DEPTH, CONV_WIDTH), 0.01),
        "conv_pw_w": nrm(ks[7], (DEPTH, CONV_WIDTH, CONV_WIDTH), CONV_WIDTH ** -0.5),
        "conv_pw_b": nrm(ks[8], (DEPTH, CONV_WIDTH), 0.01),
        "q_norm_g": 1.0 + nrm(ks[9], (DEPTH, HEAD_DIM), 0.02),
        "k_norm_g": 1.0 + nrm(ks[10], (DEPTH, HEAD_DIM), 0.02),
        "w_out": nrm(ks[11], (DEPTH, MIX_WIDTH, D_MODEL), MIX_WIDTH ** -0.5),
    }


def reference(x, norm_g, w_in, conv_w, conv_b, conv_norm_g, conv_norm_b,
              conv_pw_w, conv_pw_b, q_norm_g, k_norm_g, w_out):
    for i in range(DEPTH):
        x = hybrid_layer(x, norm_g[i], w_in[i], conv_w[i], conv_b[i],
                         conv_norm_g[i], conv_norm_b[i], conv_pw_w[i], conv_pw_b[i],
                         q_norm_g[i], k_norm_g[i], w_out[i])
    return x
```

```python
import functools

import jax
import jax.numpy as jnp
from jax import lax
from jax.experimental import pallas as pl
from jax.experimental.pallas import tpu as pltpu

D_MODEL = 2048
CONV_WIDTH = 1024
CONV_KERNEL = 31
ATTN_WIDTH = 1024
N_HEADS = 16
HEAD_DIM = 64
IN_WIDTH = 3 * CONV_WIDTH + 4 * ATTN_WIDTH
DILATED_PATTERNS = ((128, 1), (512, 4), (2048, 16))
RMS_EPS = 1e-6
LN_EPS = 1e-5
MASK_VALUE = -1e30

LANES = 128
HEADS_PER_SLAB = LANES // HEAD_DIM
N_SLABS = N_HEADS // HEADS_PER_SLAB
ATTN_BLOCK = 128
MXU_TILE = 256
HALO = 32

IN_TM = 512
OUT_TM = 512
CONV_SLABS = CONV_WIDTH // LANES
CONV_ROWS = 128
NORM_ROWS = 64
VMEM_LIMIT = 60 * 1024 * 1024

_OFF_VAL, _OFF_GLU, _OFF_AGATE = 0, CONV_WIDTH, 2 * CONV_WIDTH
_OFF_Q = 3 * CONV_WIDTH
_OFF_K, _OFF_V, _OFF_BGATE = _OFF_Q + ATTN_WIDTH, _OFF_Q + 2 * ATTN_WIDTH, _OFF_Q + 3 * ATTN_WIDTH


def _sigmoid(z):
    return 1.0 / (1.0 + jnp.exp(-z))


def _in_proj_kernel(x_ref, g_ref, w_ref, hsum_ref, qg_ref, kg_ref,
                    glu_ref, sga_ref, sgb_ref, q_ref, k_ref, v_ref, h_scr):
    x = x_ref[...]
    ms = jnp.mean(x * x, axis=-1, keepdims=True)
    h_scr[...] = (x * lax.rsqrt(ms + RMS_EPS) * g_ref[...]).astype(jnp.bfloat16)

    def proj(col):
        return jnp.dot(h_scr[...], w_ref[:, col:col + MXU_TILE],
                       preferred_element_type=jnp.float32)

    for c in range(CONV_WIDTH // MXU_TILE):
        lo = c * MXU_TILE
        sl = slice(lo, lo + MXU_TILE)
        glu_ref[:, sl] = (proj(_OFF_VAL + lo) * _sigmoid(proj(_OFF_GLU + lo))).astype(jnp.bfloat16)
        z = proj(_OFF_AGATE + lo)
        sga_ref[:, sl] = (z * _sigmoid(z)).astype(jnp.bfloat16)

    def head_norm(z, gain):
        ms = jnp.dot((z * z).astype(jnp.bfloat16), hsum_ref[...],
                     preferred_element_type=jnp.float32)
        return z * lax.rsqrt(ms + RMS_EPS) * gain

    def store_slabs(ref, c, val):
        for s in range(MXU_TILE // LANES):
            ref[c * (MXU_TILE // LANES) + s] = val[:, s * LANES:(s + 1) * LANES].astype(jnp.bfloat16)

    for c in range(ATTN_WIDTH // MXU_TILE):
        lo = c * MXU_TILE
        store_slabs(q_ref, c, head_norm(proj(_OFF_Q + lo), qg_ref[...]))
        store_slabs(k_ref, c, head_norm(proj(_OFF_K + lo), kg_ref[...]))
        store_slabs(v_ref, c, proj(_OFF_V + lo))
        z = proj(_OFF_BGATE + lo)
        sgb_ref[:, lo:lo + MXU_TILE] = (z * _sigmoid(z)).astype(jnp.bfloat16)


def _in_proj(x2, norm_g, w_in, hsum, qg, kg):
    n = x2.shape[0]
    const = lambda i: (0, 0)
    tok = lambda i: (i, 0)
    slab = lambda i: (0, i, 0)
    bf = jnp.bfloat16
    return pl.pallas_call(
        _in_proj_kernel,
        grid=(n // IN_TM,),
        in_specs=[
            pl.BlockSpec((IN_TM, D_MODEL), tok),
            pl.BlockSpec((1, D_MODEL), const),
            pl.BlockSpec((D_MODEL, IN_WIDTH), const, pipeline_mode=pl.Buffered(1)),
            pl.BlockSpec((MXU_TILE, MXU_TILE), const),
            pl.BlockSpec((1, MXU_TILE), const),
            pl.BlockSpec((1, MXU_TILE), const),
        ],
        out_specs=[
            pl.BlockSpec((IN_TM, CONV_WIDTH), tok),
            pl.BlockSpec((IN_TM, CONV_WIDTH), tok),
            pl.BlockSpec((IN_TM, ATTN_WIDTH), tok),
            pl.BlockSpec((N_SLABS, IN_TM, LANES), slab),
            pl.BlockSpec((N_SLABS, IN_TM, LANES), slab),
            pl.BlockSpec((N_SLABS, IN_TM, LANES), slab),
        ],
        out_shape=[
            jax.ShapeDtypeStruct((n, CONV_WIDTH), bf),
            jax.ShapeDtypeStruct((n, CONV_WIDTH), bf),
            jax.ShapeDtypeStruct((n, ATTN_WIDTH), bf),
            jax.ShapeDtypeStruct((N_SLABS, n, LANES), bf),
            jax.ShapeDtypeStruct((N_SLABS, n, LANES), bf),
            jax.ShapeDtypeStruct((N_SLABS, n, LANES), bf),
        ],
        scratch_shapes=[pltpu.VMEM((IN_TM, D_MODEL), bf)],
        compiler_params=pltpu.CompilerParams(
            dimension_semantics=("arbitrary",), vmem_limit_bytes=VMEM_LIMIT),
        name="in_proj",
    )(x2, norm_g, w_in, hsum, qg, kg)


def _attn_tile(q_blk, k_blk, v_blk, bias, first, head_mask):
    q2 = jnp.concatenate([q_blk * head_mask, q_blk * (1 - head_mask)], axis=0)
    s = lax.dot_general(q2, k_blk, (((1,), (1,)), ((), ())),
                        preferred_element_type=jnp.float32)
    s = s + jnp.concatenate([bias, bias], axis=0)
    m = jnp.max(s, axis=-1, keepdims=True)
    p = jnp.exp(s - m)
    l = jnp.sum(p, axis=-1, keepdims=True)
    pv = jnp.dot(p.astype(jnp.bfloat16), v_blk, preferred_element_type=jnp.float32)
    o2 = pv / l
    lse2 = jnp.broadcast_to(m + jnp.log(l), o2.shape)
    o = jnp.where(first, o2[:ATTN_BLOCK], o2[ATTN_BLOCK:])
    lse = jnp.where(first, lse2[:ATTN_BLOCK], lse2[ATTN_BLOCK:])
    return o, lse


def _attention_kernel(q_ref, k_ref, v_ref, sgb_ref, bias_ref, o_ref,
                      f32_scr, q4_scr, k4_scr, v4_scr, q16_scr, k16_scr, v16_scr,
                      o_scr, lse_scr):
    seq = q_ref.shape[0]
    lane = lax.broadcasted_iota(jnp.int32, (ATTN_BLOCK, LANES), 1)
    first = lane < HEAD_DIM
    head_mask = first.astype(jnp.float32).astype(jnp.bfloat16)

    for src, dst4, dst16 in ((q_ref, q4_scr, q16_scr), (k_ref, k4_scr, k16_scr), (v_ref, v4_scr, v16_scr)):
        f32_scr[...] = src[...].astype(jnp.float32)
        for dst in (dst4, dst16):
            dil, sub = dst.shape[0], dst.shape[1]
            for r in range(dil):
                dst[r] = f32_scr[pl.ds(r, sub, stride=dil), :].astype(jnp.bfloat16)

    def run_pattern(p_idx, dil, load_q, load_kv):
        n_blk = seq // dil // ATTN_BLOCK

        def emit(r, n, o, lse):
            row = dil * ATTN_BLOCK * n + r
            if dil == 1:
                idx = pl.ds(pl.multiple_of(row, ATTN_BLOCK), ATTN_BLOCK)
            else:
                idx = pl.ds(row, ATTN_BLOCK, stride=dil)
            o_scr[p_idx, idx, :] = o
            lse_scr[p_idx, idx, :] = lse

        def per_residue(r):
            k_blk, v_blk = load_kv(r, 0, ATTN_BLOCK)
            o, lse = _attn_tile(load_q(r, 0), k_blk, v_blk, bias_ref[:, ATTN_BLOCK:], first, head_mask)
            emit(r, 0, o, lse)

            def body(n, carry):
                row0 = pl.multiple_of(n * ATTN_BLOCK, ATTN_BLOCK)
                k_blk, v_blk = load_kv(r, row0 - ATTN_BLOCK, 2 * ATTN_BLOCK)
                o, lse = _attn_tile(load_q(r, row0), k_blk, v_blk, bias_ref[...], first, head_mask)
                emit(r, n, o, lse)
                return carry

            lax.fori_loop(1, n_blk, body, 0)

        if dil == 1:
            per_residue(0)
        else:
            lax.fori_loop(0, dil, lambda r, c: (per_residue(r), c)[1], 0)

    def dense_q(r, row0):
        return q_ref[pl.ds(row0, ATTN_BLOCK), :]

    def dense_kv(r, row0, rows):
        return k_ref[pl.ds(row0, rows), :], v_ref[pl.ds(row0, rows), :]

    def strided_loaders(qs, ks, vs):
        def load_q(r, row0):
            return qs[r, pl.ds(row0, ATTN_BLOCK), :]

        def load_kv(r, row0, rows):
            return ks[r, pl.ds(row0, rows), :], vs[r, pl.ds(row0, rows), :]
        return load_q, load_kv

    run_pattern(0, 1, dense_q, dense_kv)
    run_pattern(1, 4, *strided_loaders(q4_scr, k4_scr, v4_scr))
    run_pattern(2, 16, *strided_loaders(q16_scr, k16_scr, v16_scr))

    chunk = 256

    def mix(i, carry):
        rows = pl.ds(pl.multiple_of(i * chunk, chunk), chunk)
        lses = [lse_scr[p, rows, :] for p in range(3)]
        top = jnp.maximum(jnp.maximum(lses[0], lses[1]), lses[2])
        ws = [jnp.exp(t - top) for t in lses]
        den = ws[0] + ws[1] + ws[2]
        num = ws[0] * o_scr[0, rows, :] + ws[1] * o_scr[1, rows, :] + ws[2] * o_scr[2, rows, :]
        mixed = (num / den).astype(jnp.bfloat16)
        o_ref[rows, :] = mixed * sgb_ref[rows, :]
        return carry

    lax.fori_loop(0, seq // chunk, mix, 0)


def _attention(q, k, v, sgb, bias, batch, seq):
    bf = jnp.bfloat16
    slab = pl.BlockSpec((None, None, seq, LANES), lambda b, s: (s, b, 0, 0))
    tok = pl.BlockSpec((None, seq, LANES), lambda b, s: (b, 0, s))
    dil_scratch = [pltpu.VMEM((d, seq // d, LANES), bf) for d in (4, 16) for _ in range(3)]
    return pl.pallas_call(
        _attention_kernel,
        grid=(batch, N_SLABS),
        in_specs=[slab, slab, slab, tok,
                  pl.BlockSpec((ATTN_BLOCK, 2 * ATTN_BLOCK), lambda b, s: (0, 0))],
        out_specs=tok,
        out_shape=jax.ShapeDtypeStruct((batch, seq, ATTN_WIDTH), bf),
        scratch_shapes=[pltpu.VMEM((seq, LANES), jnp.float32)] + dil_scratch + [
            pltpu.VMEM((3, seq, LANES), jnp.float32),
            pltpu.VMEM((3, seq, LANES), jnp.float32),
        ],
        compiler_params=pltpu.CompilerParams(
            dimension_semantics=("arbitrary", "arbitrary"), vmem_limit_bytes=VMEM_LIMIT),
        name="dilated_attention",
    )(q.reshape(N_SLABS, batch, seq, LANES), k.reshape(N_SLABS, batch, seq, LANES),
      v.reshape(N_SLABS, batch, seq, LANES), sgb.reshape(batch, seq, ATTN_WIDTH), bias)


def _out_proj_kernel(tiles_per_seq, glu_ref, halo_ref, sga_ref, yb_ref, x_ref,
                     cw_ref, cb_ref, lng_ref, lnb_ref, pww_ref, pwb_ref, wout_ref,
                     y_ref, u_scr, c_scr, a_scr):
    tm = glu_ref.shape[0]
    first_tile = pl.program_id(0) % tiles_per_seq == 0
    for c in range(CONV_SLABS):
        lanes = slice(c * LANES, (c + 1) * LANES)
        halo = halo_ref[:, lanes].astype(jnp.float32)
        u_scr[c, 0:HALO, :] = jnp.where(first_tile, jnp.zeros_like(halo), halo)
        u_scr[c, HALO:HALO + tm, :] = glu_ref[:, lanes].astype(jnp.float32)

    def conv_slab(c, carry):
        for chunk in range(tm // CONV_ROWS):
            base = chunk * CONV_ROWS + HALO - (CONV_KERNEL - 1)
            acc = jnp.broadcast_to(cb_ref[c], (CONV_ROWS, LANES))
            for t in range(CONV_KERNEL):
                acc = acc + u_scr[c, base + t:base + t + CONV_ROWS, :] * cw_ref[c, t:t + 1, :]
            c_scr[c, chunk * CONV_ROWS:(chunk + 1) * CONV_ROWS, :] = acc
        return carry

    lax.fori_loop(0, CONV_SLABS, conv_slab, 0)

    def norm_chunk(i, carry):
        rows = pl.ds(pl.multiple_of(i * NORM_ROWS, NORM_ROWS), NORM_ROWS)
        parts = [c_scr[c, rows, :] for c in range(CONV_SLABS)]
        mu = jnp.sum(functools.reduce(jnp.add, parts), axis=-1, keepdims=True) * (1.0 / CONV_WIDTH)
        cens = [p - mu for p in parts]
        var = jnp.sum(functools.reduce(jnp.add, [d * d for d in cens]), axis=-1, keepdims=True) * (1.0 / CONV_WIDTH)
        inv = lax.rsqrt(var + LN_EPS)
        for c in range(CONV_SLABS):
            yn = cens[c] * inv * lng_ref[c] + lnb_ref[c]
            a_scr[rows, c * LANES:(c + 1) * LANES] = (yn * _sigmoid(yn)).astype(jnp.bfloat16)
        return carry

    lax.fori_loop(0, tm // NORM_ROWS, norm_chunk, 0)

    pw = jnp.dot(a_scr[...], pww_ref[...], preferred_element_type=jnp.float32) + pwb_ref[...]
    y_a = (pw * sga_ref[...].astype(jnp.float32)).astype(jnp.bfloat16)
    y = jnp.dot(y_a, wout_ref[0:CONV_WIDTH, :], preferred_element_type=jnp.float32)
    y = y + jnp.dot(yb_ref[...], wout_ref[CONV_WIDTH:, :], preferred_element_type=jnp.float32)
    y_ref[...] = x_ref[...] + y


def _out_proj(glu, sga, yb, x2, cw, cb, lng, lnb, pww, pwb, wout, seq):
    n = x2.shape[0]
    tiles_per_seq = seq // OUT_TM
    halo_blocks = OUT_TM // HALO
    const = lambda i: (0, 0)
    const3 = lambda i: (0, 0, 0)
    tok = lambda i: (i, 0)
    resident = functools.partial(pl.BlockSpec, index_map=const, pipeline_mode=pl.Buffered(1))
    slab_rows = lambda rows: pl.BlockSpec((CONV_SLABS, rows, LANES), const3)
    return pl.pallas_call(
        functools.partial(_out_proj_kernel, tiles_per_seq),
        grid=(n // OUT_TM,),
        in_specs=[
            pl.BlockSpec((OUT_TM, CONV_WIDTH), tok),
            pl.BlockSpec((HALO, CONV_WIDTH), lambda i: (jnp.maximum(i * halo_blocks - 1, 0), 0)),
            pl.BlockSpec((OUT_TM, CONV_WIDTH), tok),
            pl.BlockSpec((OUT_TM, ATTN_WIDTH), tok),
            pl.BlockSpec((OUT_TM, D_MODEL), tok),
            slab_rows(CONV_KERNEL),
            slab_rows(1),
            slab_rows(1),
            slab_rows(1),
            resident((CONV_WIDTH, CONV_WIDTH)),
            pl.BlockSpec((1, CONV_WIDTH), const),
            resident((D_MODEL, D_MODEL)),
        ],
        out_specs=pl.BlockSpec((OUT_TM, D_MODEL), tok),
        out_shape=jax.ShapeDtypeStruct((n, D_MODEL), jnp.float32),
        scratch_shapes=[
            pltpu.VMEM((CONV_SLABS, HALO + OUT_TM, LANES), jnp.float32),
            pltpu.VMEM((CONV_SLABS, OUT_TM, LANES), jnp.float32),
            pltpu.VMEM((OUT_TM, CONV_WIDTH), jnp.bfloat16),
        ],
        compiler_params=pltpu.CompilerParams(
            dimension_semantics=("arbitrary",), vmem_limit_bytes=VMEM_LIMIT),
        name="out_proj",
    )(glu, glu, sga, yb, x2, cw, cb, lng, lnb, pww, pwb, wout)


def _attention_bias():
    qi = jnp.arange(ATTN_BLOCK)[:, None]
    ki = jnp.arange(2 * ATTN_BLOCK)[None, :]
    delta = ATTN_BLOCK + qi - ki
    valid = (delta >= 0) & (delta <= ATTN_BLOCK)
    return jnp.where(valid, 0.0, MASK_VALUE).astype(jnp.float32)


def _head_sum_matrix():
    head = jnp.arange(MXU_TILE) // HEAD_DIM
    return jnp.where(head[:, None] == head[None, :], 1.0 / HEAD_DIM, 0.0).astype(jnp.bfloat16)


def _layer(x, norm_g, w_in, conv_w, conv_b, conv_norm_g, conv_norm_b,
           conv_pw_w, conv_pw_b, q_norm_g, k_norm_g, w_out):
    batch, seq, _ = x.shape
    assert all(w // d == ATTN_BLOCK for w, d in DILATED_PATTERNS)
    assert seq % (16 * ATTN_BLOCK) == 0 and seq % OUT_TM == 0 and (batch * seq) % IN_TM == 0
    bf = jnp.bfloat16
    row = lambda a: a.reshape(1, -1).astype(jnp.float32)
    x2 = x.reshape(batch * seq, D_MODEL)
    heads_per_tile = MXU_TILE // HEAD_DIM
    qg = row(jnp.tile(q_norm_g, heads_per_tile)) * (HEAD_DIM ** -0.5)
    kg = row(jnp.tile(k_norm_g, heads_per_tile))
    glu, sga, sgb, q, k, v = _in_proj(x2, row(norm_g), w_in.astype(bf), _head_sum_matrix(), qg, kg)
    yb = _attention(q, k, v, sgb, _attention_bias(), batch, seq)
    slabs = lambda a: a.astype(jnp.float32).reshape(-1, CONV_SLABS, LANES).transpose(1, 0, 2)
    y = _out_proj(glu, sga, yb.reshape(batch * seq, ATTN_WIDTH), x2,
                  slabs(conv_w), slabs(conv_b), slabs(conv_norm_g), slabs(conv_norm_b),
                  conv_pw_w.astype(bf), row(conv_pw_b), w_out.astype(bf), seq)
    return y.reshape(batch, seq, D_MODEL)


def kernel(x, norm_g, w_in, conv_w, conv_b, conv_norm_g, conv_norm_b, conv_pw_w, conv_pw_b,
           q_norm_g, k_norm_g, w_out):
    for i in range(norm_g.shape[0]):
        x = _layer(x, norm_g[i], w_in[i], conv_w[i], conv_b[i], conv_norm_g[i], conv_norm_b[i],
                   conv_pw_w[i], conv_pw_b[i], q_norm_g[i], k_norm_g[i], w_out[i])
    return x
```

```python
import functools

import jax
import jax.numpy as jnp
from jax import lax
from jax.experimental import pallas as pl
from jax.experimental.pallas import tpu as pltpu

D_MODEL = 2048
CONV_WIDTH = 1024
CONV_KERNEL = 31
ATTN_WIDTH = 1024
N_HEADS = 16
HEAD_DIM = 64
IN_WIDTH = 3 * CONV_WIDTH + 4 * ATTN_WIDTH
DILATED_PATTERNS = ((128, 1), (512, 4), (2048, 16))
RMS_EPS = 1e-6
LN_EPS = 1e-5
MASK_VALUE = -1e30

LANES = 128
HEADS_PER_SLAB = LANES // HEAD_DIM
N_SLABS = N_HEADS // HEADS_PER_SLAB
ATTN_BLOCK = 128
MXU_TILE = 256
HALO = 32

IN_TM = 512
OUT_TM = 512
CONV_SLABS = CONV_WIDTH // LANES
CONV_ROWS = 128
NORM_ROWS = 64
ATTN_UNROLL = 4
MIX_ROWS = 256
VMEM_LIMIT = 60 * 1024 * 1024

_OFF_VAL, _OFF_GLU, _OFF_AGATE = 0, CONV_WIDTH, 2 * CONV_WIDTH
_OFF_Q = 3 * CONV_WIDTH
_OFF_K, _OFF_V, _OFF_BGATE = _OFF_Q + ATTN_WIDTH, _OFF_Q + 2 * ATTN_WIDTH, _OFF_Q + 3 * ATTN_WIDTH


def _sigmoid(z):
    return 1.0 / (1.0 + jnp.exp(-z))


def _in_proj_kernel(x_ref, g_ref, w_ref, hsum_ref, qg_ref, kg_ref,
                    glu_ref, sga_ref, sgb_ref, q_ref, k_ref, v_ref, h_scr):
    x = x_ref[...]
    ms = jnp.mean(x * x, axis=-1, keepdims=True)
    h_scr[...] = (x * lax.rsqrt(ms + RMS_EPS) * g_ref[...]).astype(jnp.bfloat16)

    def proj(col):
        return jnp.dot(h_scr[...], w_ref[:, col:col + MXU_TILE],
                       preferred_element_type=jnp.float32)

    for c in range(CONV_WIDTH // MXU_TILE):
        lo = c * MXU_TILE
        sl = slice(lo, lo + MXU_TILE)
        glu_ref[:, sl] = (proj(_OFF_VAL + lo) * _sigmoid(proj(_OFF_GLU + lo))).astype(jnp.bfloat16)
        z = proj(_OFF_AGATE + lo)
        sga_ref[:, sl] = (z * _sigmoid(z)).astype(jnp.bfloat16)

    def head_norm(z, gain):
        ms = jnp.dot((z * z).astype(jnp.bfloat16), hsum_ref[...],
                     preferred_element_type=jnp.float32)
        return z * lax.rsqrt(ms + RMS_EPS) * gain

    def store_slabs(ref, c, val):
        for s in range(MXU_TILE // LANES):
            ref[c * (MXU_TILE // LANES) + s] = val[:, s * LANES:(s + 1) * LANES].astype(jnp.bfloat16)

    for c in range(ATTN_WIDTH // MXU_TILE):
        lo = c * MXU_TILE
        store_slabs(q_ref, c, head_norm(proj(_OFF_Q + lo), qg_ref[...]))
        store_slabs(k_ref, c, head_norm(proj(_OFF_K + lo), kg_ref[...]))
        store_slabs(v_ref, c, proj(_OFF_V + lo))
        z = proj(_OFF_BGATE + lo)
        sgb_ref[:, lo:lo + MXU_TILE] = (z * _sigmoid(z)).astype(jnp.bfloat16)


def _in_proj(x2, norm_g, w_in, hsum, qg, kg):
    n = x2.shape[0]
    const = lambda i: (0, 0)
    tok = lambda i: (i, 0)
    slab = lambda i: (0, i, 0)
    bf = jnp.bfloat16
    return pl.pallas_call(
        _in_proj_kernel,
        grid=(n // IN_TM,),
        in_specs=[
            pl.BlockSpec((IN_TM, D_MODEL), tok),
            pl.BlockSpec((1, D_MODEL), const),
            pl.BlockSpec((D_MODEL, IN_WIDTH), const, pipeline_mode=pl.Buffered(1)),
            pl.BlockSpec((MXU_TILE, MXU_TILE), const),
            pl.BlockSpec((1, MXU_TILE), const),
            pl.BlockSpec((1, MXU_TILE), const),
        ],
        out_specs=[
            pl.BlockSpec((IN_TM, CONV_WIDTH), tok),
            pl.BlockSpec((IN_TM, CONV_WIDTH), tok),
            pl.BlockSpec((IN_TM, ATTN_WIDTH), tok),
            pl.BlockSpec((N_SLABS, IN_TM, LANES), slab),
            pl.BlockSpec((N_SLABS, IN_TM, LANES), slab),
            pl.BlockSpec((N_SLABS, IN_TM, LANES), slab),
        ],
        out_shape=[
            jax.ShapeDtypeStruct((n, CONV_WIDTH), bf),
            jax.ShapeDtypeStruct((n, CONV_WIDTH), bf),
            jax.ShapeDtypeStruct((n, ATTN_WIDTH), bf),
            jax.ShapeDtypeStruct((N_SLABS, n, LANES), bf),
            jax.ShapeDtypeStruct((N_SLABS, n, LANES), bf),
            jax.ShapeDtypeStruct((N_SLABS, n, LANES), bf),
        ],
        scratch_shapes=[pltpu.VMEM((IN_TM, D_MODEL), bf)],
        compiler_params=pltpu.CompilerParams(
            dimension_semantics=("arbitrary",), vmem_limit_bytes=VMEM_LIMIT),
        name="in_proj",
    )(x2, norm_g, w_in, hsum, qg, kg)


def _attn_tile(q_blk, k_blk, vx_blk, bias, first, mask_a, mask_b):
    q2 = jnp.concatenate([q_blk * mask_a, q_blk * mask_b], axis=0)
    s = lax.dot_general(q2, k_blk, (((1,), (1,)), ((), ())),
                        preferred_element_type=jnp.float32)
    s = s + jnp.concatenate([bias, bias], axis=0)
    m = jnp.max(s, axis=-1, keepdims=True)
    p = jnp.exp(s - m).astype(jnp.bfloat16)
    pv = jnp.dot(p, vx_blk, preferred_element_type=jnp.float32)
    m_b = jnp.broadcast_to(m, (2 * ATTN_BLOCK, LANES))
    acc = jnp.where(first, pv[:ATTN_BLOCK, :LANES], pv[ATTN_BLOCK:, :LANES])
    den = jnp.where(first, pv[:ATTN_BLOCK, LANES:], pv[ATTN_BLOCK:, LANES:])
    top = jnp.where(first, m_b[:ATTN_BLOCK], m_b[ATTN_BLOCK:])
    return acc / den, top + jnp.log(den)


def _attention_kernel(q_ref, k_ref, v_ref, sgb_ref, bias_ref, o_ref,
                      f32_scr, a4_scr, vx1_scr, q4_scr, k4_scr, vx4_scr, q16_scr, k16_scr, vx16_scr,
                      o_scr, lse_scr):
    seq = q_ref.shape[0]
    bf = jnp.bfloat16
    lane = lax.broadcasted_iota(jnp.int32, (ATTN_BLOCK, LANES), 1)
    first = lane < HEAD_DIM
    mask_a = first.astype(jnp.float32).astype(bf)
    mask_b = 1 - mask_a

    @pl.when((pl.program_id(0) == 0) & (pl.program_id(1) == 0))
    def _():
        vx1_scr[:, LANES:] = jnp.ones((seq, LANES), bf)
        vx4_scr[:, :, LANES:] = jnp.ones((4, seq // 4, LANES), bf)
        vx16_scr[:, :, LANES:] = jnp.ones((16, seq // 16, LANES), bf)

    vx1_scr[:, :LANES] = v_ref[...]

    for src, dst4, dst16 in ((q_ref, q4_scr, q16_scr), (k_ref, k4_scr, k16_scr), (v_ref, vx4_scr, vx16_scr)):
        f32_scr[...] = src[...].astype(jnp.float32)
        for r4 in range(4):
            part = f32_scr[pl.ds(r4, seq // 4, stride=4), :]
            a4_scr[r4] = part
            dst4[r4, :, :LANES] = part.astype(bf)
        for r4 in range(4):
            for hi in range(4):
                dst16[r4 + 4 * hi, :, :LANES] = a4_scr[r4, pl.ds(hi, seq // 16, stride=4), :].astype(bf)

    n_tiles = seq // ATTN_BLOCK

    def run_pattern(p_idx, dil, q_src, k_src, vx_src):
        blk_per_res = n_tiles // dil

        def tile(i):
            res, n = i // blk_per_res, i % blk_per_res
            row0 = pl.multiple_of(n * ATTN_BLOCK, ATTN_BLOCK)
            key0 = pl.multiple_of(jnp.maximum(row0 - ATTN_BLOCK, 0), ATTN_BLOCK)
            keys = pl.ds(key0, 2 * ATTN_BLOCK)
            o, lse = _attn_tile(q_src(res, pl.ds(row0, ATTN_BLOCK)), k_src(res, keys), vx_src(res, keys),
                                bias_ref[jnp.minimum(n, 1)], first, mask_a, mask_b)
            out_row = dil * row0 + res
            if dil == 1:
                rows = pl.ds(pl.multiple_of(out_row, ATTN_BLOCK), ATTN_BLOCK)
            else:
                rows = pl.ds(out_row, ATTN_BLOCK, stride=dil)
            o_scr[p_idx, rows, :] = o
            lse_scr[p_idx, rows, :] = lse

        def body(j, carry):
            for u in range(ATTN_UNROLL):
                tile(j * ATTN_UNROLL + u)
            return carry

        lax.fori_loop(0, n_tiles // ATTN_UNROLL, body, 0)

    run_pattern(0, 1, lambda r, rows: q_ref[rows, :], lambda r, rows: k_ref[rows, :],
                lambda r, rows: vx1_scr[rows, :])
    run_pattern(1, 4, lambda r, rows: q4_scr[r, rows, :], lambda r, rows: k4_scr[r, rows, :],
                lambda r, rows: vx4_scr[r, rows, :])
    run_pattern(2, 16, lambda r, rows: q16_scr[r, rows, :], lambda r, rows: k16_scr[r, rows, :],
                lambda r, rows: vx16_scr[r, rows, :])

    def mix(i, carry):
        rows = pl.ds(pl.multiple_of(i * MIX_ROWS, MIX_ROWS), MIX_ROWS)
        lses = [lse_scr[p, rows, :] for p in range(3)]
        top = jnp.maximum(jnp.maximum(lses[0], lses[1]), lses[2])
        ws = [jnp.exp(t - top) for t in lses]
        den = ws[0] + ws[1] + ws[2]
        num = ws[0] * o_scr[0, rows, :] + ws[1] * o_scr[1, rows, :] + ws[2] * o_scr[2, rows, :]
        mixed = (num / den).astype(bf)
        o_ref[rows, :] = mixed * sgb_ref[rows, :]
        return carry

    lax.fori_loop(0, seq // MIX_ROWS, mix, 0)


def _attention(q, k, v, sgb, bias, batch, seq):
    bf = jnp.bfloat16
    slab = pl.BlockSpec((None, None, seq, LANES), lambda b, s: (s, b, 0, 0))
    tok = pl.BlockSpec((None, seq, LANES), lambda b, s: (b, 0, s))
    residue_major = lambda d, lanes: pltpu.VMEM((d, seq // d, lanes), bf)
    return pl.pallas_call(
        _attention_kernel,
        grid=(batch, N_SLABS),
        in_specs=[slab, slab, slab, tok,
                  pl.BlockSpec((2, ATTN_BLOCK, 2 * ATTN_BLOCK), lambda b, s: (0, 0, 0))],
        out_specs=tok,
        out_shape=jax.ShapeDtypeStruct((batch, seq, ATTN_WIDTH), bf),
        scratch_shapes=[
            pltpu.VMEM((seq, LANES), jnp.float32),
            pltpu.VMEM((4, seq // 4, LANES), jnp.float32),
            pltpu.VMEM((seq, 2 * LANES), bf),
            residue_major(4, LANES), residue_major(4, LANES), residue_major(4, 2 * LANES),
            residue_major(16, LANES), residue_major(16, LANES), residue_major(16, 2 * LANES),
            pltpu.VMEM((3, seq, LANES), jnp.float32),
            pltpu.VMEM((3, seq, LANES), jnp.float32),
        ],
        compiler_params=pltpu.CompilerParams(
            dimension_semantics=("arbitrary", "arbitrary"), vmem_limit_bytes=VMEM_LIMIT),
        name="dilated_attention",
    )(q.reshape(N_SLABS, batch, seq, LANES), k.reshape(N_SLABS, batch, seq, LANES),
      v.reshape(N_SLABS, batch, seq, LANES), sgb.reshape(batch, seq, ATTN_WIDTH), bias)


def _out_proj_kernel(tiles_per_seq, glu_ref, halo_ref, sga_ref, yb_ref, x_ref,
                     cw_ref, cb_ref, lng_ref, lnb_ref, pww_ref, pwb_ref, wout_ref,
                     y_ref, u_scr, c_scr, a_scr):
    tm = glu_ref.shape[0]
    first_tile = pl.program_id(0) % tiles_per_seq == 0
    for c in range(CONV_SLABS):
        lanes = slice(c * LANES, (c + 1) * LANES)
        halo = halo_ref[:, lanes].astype(jnp.float32)
        u_scr[c, 0:HALO, :] = jnp.where(first_tile, jnp.zeros_like(halo), halo)
        u_scr[c, HALO:HALO + tm, :] = glu_ref[:, lanes].astype(jnp.float32)

    def conv_slab(c, carry):
        for chunk in range(tm // CONV_ROWS):
            base = chunk * CONV_ROWS + HALO - (CONV_KERNEL - 1)
            acc = jnp.broadcast_to(cb_ref[c], (CONV_ROWS, LANES))
            for t in range(CONV_KERNEL):
                acc = acc + u_scr[c, base + t:base + t + CONV_ROWS, :] * cw_ref[c, t:t + 1, :]
            c_scr[c, chunk * CONV_ROWS:(chunk + 1) * CONV_ROWS, :] = acc
        return carry

    lax.fori_loop(0, CONV_SLABS, conv_slab, 0)

    def norm_chunk(i, carry):
        rows = pl.ds(pl.multiple_of(i * NORM_ROWS, NORM_ROWS), NORM_ROWS)
        parts = [c_scr[c, rows, :] for c in range(CONV_SLABS)]
        mu = jnp.sum(functools.reduce(jnp.add, parts), axis=-1, keepdims=True) * (1.0 / CONV_WIDTH)
        cens = [p - mu for p in parts]
        var = jnp.sum(functools.reduce(jnp.add, [d * d for d in cens]), axis=-1, keepdims=True) * (1.0 / CONV_WIDTH)
        inv = lax.rsqrt(var + LN_EPS)
        for c in range(CONV_SLABS):
            yn = cens[c] * inv * lng_ref[c] + lnb_ref[c]
            a_scr[rows, c * LANES:(c + 1) * LANES] = (yn * _sigmoid(yn)).astype(jnp.bfloat16)
        return carry

    lax.fori_loop(0, tm // NORM_ROWS, norm_chunk, 0)

    pw = jnp.dot(a_scr[...], pww_ref[...], preferred_element_type=jnp.float32) + pwb_ref[...]
    y_a = (pw * sga_ref[...].astype(jnp.float32)).astype(jnp.bfloat16)
    y = jnp.dot(y_a, wout_ref[0:CONV_WIDTH, :], preferred_element_type=jnp.float32)
    y = y + jnp.dot(yb_ref[...], wout_ref[CONV_WIDTH:, :], preferred_element_type=jnp.float32)
    y_ref[...] = x_ref[...] + y


def _out_proj(glu, sga, yb, x2, cw, cb, lng, lnb, pww, pwb, wout, seq):
    n = x2.shape[0]
    tiles_per_seq = seq // OUT_TM
    halo_blocks = OUT_TM // HALO
    const = lambda i: (0, 0)
    const3 = lambda i: (0, 0, 0)
    tok = lambda i: (i, 0)
    resident = functools.partial(pl.BlockSpec, index_map=const, pipeline_mode=pl.Buffered(1))
    slab_rows = lambda rows: pl.BlockSpec((CONV_SLABS, rows, LANES), const3)
    return pl.pallas_call(
        functools.partial(_out_proj_kernel, tiles_per_seq),
        grid=(n // OUT_TM,),
        in_specs=[
            pl.BlockSpec((OUT_TM, CONV_WIDTH), tok),
            pl.BlockSpec((HALO, CONV_WIDTH), lambda i: (jnp.maximum(i * halo_blocks - 1, 0), 0)),
            pl.BlockSpec((OUT_TM, CONV_WIDTH), tok),
            pl.BlockSpec((OUT_TM, ATTN_WIDTH), tok),
            pl.BlockSpec((OUT_TM, D_MODEL), tok),
            slab_rows(CONV_KERNEL),
            slab_rows(1),
            slab_rows(1),
            slab_rows(1),
            resident((CONV_WIDTH, CONV_WIDTH)),
            pl.BlockSpec((1, CONV_WIDTH), const),
            resident((D_MODEL, D_MODEL)),
        ],
        out_specs=pl.BlockSpec((OUT_TM, D_MODEL), tok),
        out_shape=jax.ShapeDtypeStruct((n, D_MODEL), jnp.float32),
        scratch_shapes=[
            pltpu.VMEM((CONV_SLABS, HALO + OUT_TM, LANES), jnp.float32),
            pltpu.VMEM((CONV_SLABS, OUT_TM, LANES), jnp.float32),
            pltpu.VMEM((OUT_TM, CONV_WIDTH), jnp.bfloat16),
        ],
        compiler_params=pltpu.CompilerParams(
            dimension_semantics=("arbitrary",), vmem_limit_bytes=VMEM_LIMIT),
        name="out_proj",
    )(glu, glu, sga, yb, x2, cw, cb, lng, lnb, pww, pwb, wout)


def _attention_bias():
    qi = jnp.arange(ATTN_BLOCK)[:, None]
    ki = jnp.arange(2 * ATTN_BLOCK)[None, :]
    delta = ATTN_BLOCK + qi - ki
    later = (delta >= 0) & (delta <= ATTN_BLOCK)
    first = ki <= qi
    return jnp.where(jnp.stack([first, later]), 0.0, MASK_VALUE).astype(jnp.float32)


def _head_sum_matrix():
    head = jnp.arange(MXU_TILE) // HEAD_DIM
    return jnp.where(head[:, None] == head[None, :], 1.0 / HEAD_DIM, 0.0).astype(jnp.bfloat16)


def _layer(x, norm_g, w_in, conv_w, conv_b, conv_norm_g, conv_norm_b,
           conv_pw_w, conv_pw_b, q_norm_g, k_norm_g, w_out):
    batch, seq, _ = x.shape
    assert all(w // d == ATTN_BLOCK for w, d in DILATED_PATTERNS)
    assert seq % (16 * ATTN_BLOCK) == 0 and seq % OUT_TM == 0 and (batch * seq) % IN_TM == 0
    bf = jnp.bfloat16
    row = lambda a: a.reshape(1, -1).astype(jnp.float32)
    x2 = x.reshape(batch * seq, D_MODEL)
    heads_per_tile = MXU_TILE // HEAD_DIM
    qg = row(jnp.tile(q_norm_g, heads_per_tile)) * (HEAD_DIM ** -0.5)
    kg = row(jnp.tile(k_norm_g, heads_per_tile))
    glu, sga, sgb, q, k, v = _in_proj(x2, row(norm_g), w_in.astype(bf), _head_sum_matrix(), qg, kg)
    yb = _attention(q, k, v, sgb, _attention_bias(), batch, seq)
    slabs = lambda a: a.astype(jnp.float32).reshape(-1, CONV_SLABS, LANES).transpose(1, 0, 2)
    y = _out_proj(glu, sga, yb.reshape(batch * seq, ATTN_WIDTH), x2,
                  slabs(conv_w), slabs(conv_b), slabs(conv_norm_g), slabs(conv_norm_b),
                  conv_pw_w.astype(bf), row(conv_pw_b), w_out.astype(bf), seq)
    return y.reshape(batch, seq, D_MODEL)


def kernel(x, norm_g, w_in, conv_w, conv_b, conv_norm_g, conv_norm_b, conv_pw_w, conv_pw_b,
           q_norm_g, k_norm_g, w_out):
    for i in range(norm_g.shape[0]):
        x = _layer(x, norm_g[i], w_in[i], conv_w[i], conv_b[i], conv_norm_g[i], conv_norm_b[i],
                   conv_pw_w[i], conv_pw_b[i], q_norm_g[i], k_norm_g[i], w_out[i])
    return x
```

```python
import functools

import jax
import jax.numpy as jnp
from jax import lax
from jax.experimental import pallas as pl
from jax.experimental.pallas import tpu as pltpu

D_MODEL = 2048
CONV_WIDTH = 1024
CONV_KERNEL = 31
ATTN_WIDTH = 1024
N_HEADS = 16
HEAD_DIM = 64
IN_WIDTH = 3 * CONV_WIDTH + 4 * ATTN_WIDTH
DILATED_PATTERNS = ((128, 1), (512, 4), (2048, 16))
RMS_EPS = 1e-6
LN_EPS = 1e-5
MASK_VALUE = -1e30
LOG2_E = 1.4426950408889634

LANES = 128
HEADS_PER_SLAB = LANES // HEAD_DIM
N_SLABS = N_HEADS // HEADS_PER_SLAB
ATTN_BLOCK = 128
MXU_TILE = 256
HALO = 32

IN_TM = 512
OUT_TM = 512
CONV_SLABS = CONV_WIDTH // LANES
CONV_ROWS = 128
NORM_ROWS = 64
MIX_ROWS = 256
VMEM_LIMIT = 60 * 1024 * 1024

_OFF_VAL, _OFF_GLU, _OFF_AGATE = 0, CONV_WIDTH, 2 * CONV_WIDTH
_OFF_Q = 3 * CONV_WIDTH
_OFF_K, _OFF_V, _OFF_BGATE = _OFF_Q + ATTN_WIDTH, _OFF_Q + 2 * ATTN_WIDTH, _OFF_Q + 3 * ATTN_WIDTH


def _sigmoid(z):
    return 1.0 / (1.0 + jnp.exp(-z))


def _in_proj_kernel(x_ref, g_ref, w_ref, hsum_ref, qg_ref, kg_ref,
                    glu_ref, sga_ref, sgb_ref, q_ref, k_ref, v_ref, h_scr):
    x = x_ref[...]
    ms = jnp.mean(x * x, axis=-1, keepdims=True)
    h_scr[...] = (x * lax.rsqrt(ms + RMS_EPS) * g_ref[...]).astype(jnp.bfloat16)

    def proj(col):
        return jnp.dot(h_scr[...], w_ref[:, col:col + MXU_TILE],
                       preferred_element_type=jnp.float32)

    for c in range(CONV_WIDTH // MXU_TILE):
        lo = c * MXU_TILE
        sl = slice(lo, lo + MXU_TILE)
        glu_ref[:, sl] = (proj(_OFF_VAL + lo) * _sigmoid(proj(_OFF_GLU + lo))).astype(jnp.bfloat16)
        z = proj(_OFF_AGATE + lo)
        sga_ref[:, sl] = (z * _sigmoid(z)).astype(jnp.bfloat16)

    def head_norm(z, gain):
        ms = jnp.dot((z * z).astype(jnp.bfloat16), hsum_ref[...],
                     preferred_element_type=jnp.float32)
        return z * lax.rsqrt(ms + RMS_EPS) * gain

    def store_slabs(ref, c, val):
        for s in range(MXU_TILE // LANES):
            ref[c * (MXU_TILE // LANES) + s] = val[:, s * LANES:(s + 1) * LANES].astype(jnp.bfloat16)

    for c in range(ATTN_WIDTH // MXU_TILE):
        lo = c * MXU_TILE
        zq, zk = proj(_OFF_Q + lo), proj(_OFF_K + lo)
        nq, nk = head_norm(zq, qg_ref[...]), head_norm(zk, kg_ref[...])
        store_slabs(q_ref, c, nq)
        store_slabs(k_ref, c, nk)
        zv, zb = proj(_OFF_V + lo), proj(_OFF_BGATE + lo)
        store_slabs(v_ref, c, zv)
        sgb_ref[:, lo:lo + MXU_TILE] = (zb * _sigmoid(zb)).astype(jnp.bfloat16)


def _in_proj(x2, norm_g, w_in, hsum, qg, kg):
    n = x2.shape[0]
    const = lambda i: (0, 0)
    tok = lambda i: (i, 0)
    slab = lambda i: (0, i, 0)
    bf = jnp.bfloat16
    return pl.pallas_call(
        _in_proj_kernel,
        grid=(n // IN_TM,),
        in_specs=[
            pl.BlockSpec((IN_TM, D_MODEL), tok),
            pl.BlockSpec((1, D_MODEL), const),
            pl.BlockSpec((D_MODEL, IN_WIDTH), const, pipeline_mode=pl.Buffered(1)),
            pl.BlockSpec((MXU_TILE, MXU_TILE), const),
            pl.BlockSpec((1, MXU_TILE), const),
            pl.BlockSpec((1, MXU_TILE), const),
        ],
        out_specs=[
            pl.BlockSpec((IN_TM, CONV_WIDTH), tok),
            pl.BlockSpec((IN_TM, CONV_WIDTH), tok),
            pl.BlockSpec((IN_TM, ATTN_WIDTH), tok),
            pl.BlockSpec((N_SLABS, IN_TM, LANES), slab),
            pl.BlockSpec((N_SLABS, IN_TM, LANES), slab),
            pl.BlockSpec((N_SLABS, IN_TM, LANES), slab),
        ],
        out_shape=[
            jax.ShapeDtypeStruct((n, CONV_WIDTH), bf),
            jax.ShapeDtypeStruct((n, CONV_WIDTH), bf),
            jax.ShapeDtypeStruct((n, ATTN_WIDTH), bf),
            jax.ShapeDtypeStruct((N_SLABS, n, LANES), bf),
            jax.ShapeDtypeStruct((N_SLABS, n, LANES), bf),
            jax.ShapeDtypeStruct((N_SLABS, n, LANES), bf),
        ],
        scratch_shapes=[pltpu.VMEM((IN_TM, D_MODEL), bf)],
        compiler_params=pltpu.CompilerParams(
            dimension_semantics=("arbitrary",), vmem_limit_bytes=VMEM_LIMIT),
        name="in_proj",
    )(x2, norm_g, w_in, hsum, qg, kg)


def _score_stage(q_blk, k_blk, bias, first, mask_a, mask_b):
    q2 = jnp.concatenate([q_blk * mask_a, q_blk * mask_b], axis=0)
    s = lax.dot_general(q2, k_blk, (((1,), (1,)), ((), ())),
                        preferred_element_type=jnp.float32)
    s = s + jnp.concatenate([bias, bias], axis=0)
    m = jnp.max(s, axis=-1, keepdims=True)
    p = jnp.exp2(s - m).astype(jnp.bfloat16)
    m_b = jnp.broadcast_to(m, (2 * ATTN_BLOCK, LANES))
    return p, jnp.where(first, m_b[:ATTN_BLOCK], m_b[ATTN_BLOCK:])


def _value_stage(p, vx_blk, first):
    pv = jnp.dot(p, vx_blk, preferred_element_type=jnp.float32)
    acc = jnp.where(first, pv[:ATTN_BLOCK, :LANES], pv[ATTN_BLOCK:, :LANES])
    den = jnp.where(first, pv[:ATTN_BLOCK, LANES:], pv[ATTN_BLOCK:, LANES:])
    return acc, den


def _attention_kernel(q_ref, k_ref, v_ref, sgb_ref, bias_ref, o_ref,
                      f32_scr, a4_scr, vx1_scr, q4_scr, k4_scr, vx4_scr, q16_scr, k16_scr, vx16_scr,
                      acc_scr, den_scr, top_scr):
    seq = q_ref.shape[0]
    bf = jnp.bfloat16
    lane = lax.broadcasted_iota(jnp.int32, (ATTN_BLOCK, LANES), 1)
    first = lane < HEAD_DIM
    mask_a = first.astype(jnp.float32).astype(bf)
    mask_b = 1 - mask_a

    @pl.when((pl.program_id(0) == 0) & (pl.program_id(1) == 0))
    def _():
        vx1_scr[:, LANES:] = jnp.ones((seq, LANES), bf)
        vx4_scr[:, :, LANES:] = jnp.ones((4, seq // 4, LANES), bf)
        vx16_scr[:, :, LANES:] = jnp.ones((16, seq // 16, LANES), bf)

    vx1_scr[:, :LANES] = v_ref[...]

    for src, dst4, dst16 in ((q_ref, q4_scr, q16_scr), (k_ref, k4_scr, k16_scr), (v_ref, vx4_scr, vx16_scr)):
        f32_scr[...] = src[...].astype(jnp.float32)
        for r4 in range(4):
            part = f32_scr[pl.ds(r4, seq // 4, stride=4), :]
            a4_scr[r4] = part
            dst4[r4, :, :LANES] = part.astype(bf)
        for r4 in range(4):
            for hi in range(4):
                dst16[r4 + 4 * hi, :, :LANES] = a4_scr[r4, pl.ds(hi, seq // 16, stride=4), :].astype(bf)

    n_tiles = seq // ATTN_BLOCK

    patterns = (
        (1, lambda r, rows: q_ref[rows, :], lambda r, rows: k_ref[rows, :], lambda r, rows: vx1_scr[rows, :]),
        (4, lambda r, rows: q4_scr[r, rows, :], lambda r, rows: k4_scr[r, rows, :],
         lambda r, rows: vx4_scr[r, rows, :]),
        (16, lambda r, rows: q16_scr[r, rows, :], lambda r, rows: k16_scr[r, rows, :],
         lambda r, rows: vx16_scr[r, rows, :]),
    )

    for p_idx, (dil, q_src, k_src, vx_src) in enumerate(patterns):
        blk_per_res = n_tiles // dil
        for i in range(n_tiles):
            res, n = divmod(i, blk_per_res)
            row0 = n * ATTN_BLOCK
            k_rows = pl.ds(max(row0 - ATTN_BLOCK, 0), 2 * ATTN_BLOCK)
            p, top = _score_stage(q_src(res, pl.ds(row0, ATTN_BLOCK)), k_src(res, k_rows),
                                  bias_ref[min(n, 1)], first, mask_a, mask_b)
            acc, den = _value_stage(p, vx_src(res, k_rows), first)
            out = pl.ds(dil * row0 + res, ATTN_BLOCK, stride=dil) if dil > 1 else pl.ds(row0, ATTN_BLOCK)
            top_scr[p_idx, out, :] = top
            acc_scr[p_idx, out, :] = acc
            den_scr[p_idx, out, :] = den

    def mix(i, carry):
        rows = pl.ds(pl.multiple_of(i * MIX_ROWS, MIX_ROWS), MIX_ROWS)
        tops = [top_scr[p, rows, :] for p in range(3)]
        top = jnp.maximum(jnp.maximum(tops[0], tops[1]), tops[2])
        ws = [jnp.exp2(t - top) for t in tops]
        den = ws[0] * den_scr[0, rows, :] + ws[1] * den_scr[1, rows, :] + ws[2] * den_scr[2, rows, :]
        num = ws[0] * acc_scr[0, rows, :] + ws[1] * acc_scr[1, rows, :] + ws[2] * acc_scr[2, rows, :]
        mixed = (num / den).astype(bf)
        o_ref[rows, :] = mixed * sgb_ref[rows, :]
        return carry

    lax.fori_loop(0, seq // MIX_ROWS, mix, 0)


def _attention(q, k, v, sgb, bias, batch, seq):
    bf = jnp.bfloat16
    slab = pl.BlockSpec((None, None, seq, LANES), lambda b, s: (s, b, 0, 0))
    tok = pl.BlockSpec((None, seq, LANES), lambda b, s: (b, 0, s))
    residue_major = lambda d, lanes: pltpu.VMEM((d, seq // d, lanes), bf)
    return pl.pallas_call(
        _attention_kernel,
        grid=(batch, N_SLABS),
        in_specs=[slab, slab, slab, tok,
                  pl.BlockSpec((2, ATTN_BLOCK, 2 * ATTN_BLOCK), lambda b, s: (0, 0, 0))],
        out_specs=tok,
        out_shape=jax.ShapeDtypeStruct((batch, seq, ATTN_WIDTH), bf),
        scratch_shapes=[
            pltpu.VMEM((seq, LANES), jnp.float32),
            pltpu.VMEM((4, seq // 4, LANES), jnp.float32),
            pltpu.VMEM((seq, 2 * LANES), bf),
            residue_major(4, LANES), residue_major(4, LANES), residue_major(4, 2 * LANES),
            residue_major(16, LANES), residue_major(16, LANES), residue_major(16, 2 * LANES),
            pltpu.VMEM((3, seq, LANES), jnp.float32),
            pltpu.VMEM((3, seq, LANES), jnp.float32),
            pltpu.VMEM((3, seq, LANES), jnp.float32),
        ],
        compiler_params=pltpu.CompilerParams(
            dimension_semantics=("arbitrary", "arbitrary"), vmem_limit_bytes=VMEM_LIMIT),
        name="dilated_attention",
    )(q.reshape(N_SLABS, batch, seq, LANES), k.reshape(N_SLABS, batch, seq, LANES),
      v.reshape(N_SLABS, batch, seq, LANES), sgb.reshape(batch, seq, ATTN_WIDTH), bias)


def _out_proj_kernel(tiles_per_seq, glu_ref, halo_ref, sga_ref, yb_ref, x_ref,
                     cw_ref, cb_ref, lng_ref, lnb_ref, pww_ref, pwb_ref, wout_ref,
                     y_ref, u_scr, c_scr, a_scr):
    tm = glu_ref.shape[0]
    first_tile = pl.program_id(0) % tiles_per_seq == 0
    for c in range(CONV_SLABS):
        lanes = slice(c * LANES, (c + 1) * LANES)
        halo = halo_ref[:, lanes].astype(jnp.float32)
        u_scr[c, 0:HALO, :] = jnp.where(first_tile, jnp.zeros_like(halo), halo)
        u_scr[c, HALO:HALO + tm, :] = glu_ref[:, lanes].astype(jnp.float32)

    def conv_slab(c, carry):
        for chunk in range(tm // CONV_ROWS):
            base = chunk * CONV_ROWS + HALO - (CONV_KERNEL - 1)
            acc = jnp.broadcast_to(cb_ref[c], (CONV_ROWS, LANES))
            for t in range(CONV_KERNEL):
                acc = acc + u_scr[c, base + t:base + t + CONV_ROWS, :] * cw_ref[c, t:t + 1, :]
            c_scr[c, chunk * CONV_ROWS:(chunk + 1) * CONV_ROWS, :] = acc
        return carry

    lax.fori_loop(0, CONV_SLABS, conv_slab, 0)

    def norm_chunk(i, carry):
        rows = pl.ds(pl.multiple_of(i * NORM_ROWS, NORM_ROWS), NORM_ROWS)
        parts = [c_scr[c, rows, :] for c in range(CONV_SLABS)]
        mu = jnp.sum(functools.reduce(jnp.add, parts), axis=-1, keepdims=True) * (1.0 / CONV_WIDTH)
        cens = [p - mu for p in parts]
        var = jnp.sum(functools.reduce(jnp.add, [d * d for d in cens]), axis=-1, keepdims=True) * (1.0 / CONV_WIDTH)
        inv = lax.rsqrt(var + LN_EPS)
        for c in range(CONV_SLABS):
            yn = cens[c] * inv * lng_ref[c] + lnb_ref[c]
            a_scr[rows, c * LANES:(c + 1) * LANES] = (yn * _sigmoid(yn)).astype(jnp.bfloat16)
        return carry

    lax.fori_loop(0, tm // NORM_ROWS, norm_chunk, 0)

    pw = jnp.dot(a_scr[...], pww_ref[...], preferred_element_type=jnp.float32) + pwb_ref[...]
    y_a = (pw * sga_ref[...].astype(jnp.float32)).astype(jnp.bfloat16)
    y = jnp.dot(y_a, wout_ref[0:CONV_WIDTH, :], preferred_element_type=jnp.float32)
    y = y + jnp.dot(yb_ref[...], wout_ref[CONV_WIDTH:, :], preferred_element_type=jnp.float32)
    y_ref[...] = x_ref[...] + y


def _out_proj(glu, sga, yb, x2, cw, cb, lng, lnb, pww, pwb, wout, seq):
    n = x2.shape[0]
    tiles_per_seq = seq // OUT_TM
    halo_blocks = OUT_TM // HALO
    const = lambda i: (0, 0)
    const3 = lambda i: (0, 0, 0)
    tok = lambda i: (i, 0)
    resident = functools.partial(pl.BlockSpec, index_map=const, pipeline_mode=pl.Buffered(1))
    slab_rows = lambda rows: pl.BlockSpec((CONV_SLABS, rows, LANES), const3)
    return pl.pallas_call(
        functools.partial(_out_proj_kernel, tiles_per_seq),
        grid=(n // OUT_TM,),
        in_specs=[
            pl.BlockSpec((OUT_TM, CONV_WIDTH), tok),
            pl.BlockSpec((HALO, CONV_WIDTH), lambda i: (jnp.maximum(i * halo_blocks - 1, 0), 0)),
            pl.BlockSpec((OUT_TM, CONV_WIDTH), tok),
            pl.BlockSpec((OUT_TM, ATTN_WIDTH), tok),
            pl.BlockSpec((OUT_TM, D_MODEL), tok),
            slab_rows(CONV_KERNEL),
            slab_rows(1),
            slab_rows(1),
            slab_rows(1),
            resident((CONV_WIDTH, CONV_WIDTH)),
            pl.BlockSpec((1, CONV_WIDTH), const),
            resident((D_MODEL, D_MODEL)),
        ],
        out_specs=pl.BlockSpec((OUT_TM, D_MODEL), tok),
        out_shape=jax.ShapeDtypeStruct((n, D_MODEL), jnp.float32),
        scratch_shapes=[
            pltpu.VMEM((CONV_SLABS, HALO + OUT_TM, LANES), jnp.float32),
            pltpu.VMEM((CONV_SLABS, OUT_TM, LANES), jnp.float32),
            pltpu.VMEM((OUT_TM, CONV_WIDTH), jnp.bfloat16),
        ],
        compiler_params=pltpu.CompilerParams(
            dimension_semantics=("arbitrary",), vmem_limit_bytes=VMEM_LIMIT),
        name="out_proj",
    )(glu, glu, sga, yb, x2, cw, cb, lng, lnb, pww, pwb, wout)


def _attention_bias():
    qi = jnp.arange(ATTN_BLOCK)[:, None]
    ki = jnp.arange(2 * ATTN_BLOCK)[None, :]
    delta = ATTN_BLOCK + qi - ki
    later = (delta >= 0) & (delta <= ATTN_BLOCK)
    first = ki <= qi
    return jnp.where(jnp.stack([first, later]), 0.0, MASK_VALUE).astype(jnp.float32)


def _head_sum_matrix():
    head = jnp.arange(MXU_TILE) // HEAD_DIM
    return jnp.where(head[:, None] == head[None, :], 1.0 / HEAD_DIM, 0.0).astype(jnp.bfloat16)


def _layer(x, norm_g, w_in, conv_w, conv_b, conv_norm_g, conv_norm_b,
           conv_pw_w, conv_pw_b, q_norm_g, k_norm_g, w_out):
    batch, seq, _ = x.shape
    assert all(w // d == ATTN_BLOCK for w, d in DILATED_PATTERNS)
    assert seq % (16 * ATTN_BLOCK) == 0 and seq % OUT_TM == 0 and (batch * seq) % IN_TM == 0
    bf = jnp.bfloat16
    row = lambda a: a.reshape(1, -1).astype(jnp.float32)
    x2 = x.reshape(batch * seq, D_MODEL)
    heads_per_tile = MXU_TILE // HEAD_DIM
    qg = row(jnp.tile(q_norm_g, heads_per_tile)) * (HEAD_DIM ** -0.5 * LOG2_E)
    kg = row(jnp.tile(k_norm_g, heads_per_tile))
    glu, sga, sgb, q, k, v = _in_proj(x2, row(norm_g), w_in.astype(bf), _head_sum_matrix(), qg, kg)
    yb = _attention(q, k, v, sgb, _attention_bias(), batch, seq)
    slabs = lambda a: a.astype(jnp.float32).reshape(-1, CONV_SLABS, LANES).transpose(1, 0, 2)
    y = _out_proj(glu, sga, yb.reshape(batch * seq, ATTN_WIDTH), x2,
                  slabs(conv_w), slabs(conv_b), slabs(conv_norm_g), slabs(conv_norm_b),
                  conv_pw_w.astype(bf), row(conv_pw_b), w_out.astype(bf), seq)
    return y.reshape(batch, seq, D_MODEL)


def kernel(x, norm_g, w_in, conv_w, conv_b, conv_norm_g, conv_norm_b, conv_pw_w, conv_pw_b,
           q_norm_g, k_norm_g, w_out):
    for i in range(norm_g.shape[0]):
        x = _layer(x, norm_g[i], w_in[i], conv_w[i], conv_b[i], conv_norm_g[i], conv_norm_b[i],
                   conv_pw_w[i], conv_pw_b[i], q_norm_g[i], k_norm_g[i], w_out[i])
    return x
```

```python
import functools

import jax
import jax.numpy as jnp
from jax import lax
from jax.experimental import pallas as pl
from jax.experimental.pallas import tpu as pltpu

D_MODEL = 2048
CONV_WIDTH = 1024
CONV_KERNEL = 31
ATTN_WIDTH = 1024
N_HEADS = 16
HEAD_DIM = 64
IN_WIDTH = 3 * CONV_WIDTH + 4 * ATTN_WIDTH
DILATED_PATTERNS = ((128, 1), (512, 4), (2048, 16))
RMS_EPS = 1e-6
LN_EPS = 1e-5
MASK_VALUE = -1e30
LOG2_E = 1.4426950408889634
SCORE_BOUND_SLACK = 1.02
MAX_SHARED_SHIFT = 40.0

LANES = 128
HEADS_PER_SLAB = LANES // HEAD_DIM
N_SLABS = N_HEADS // HEADS_PER_SLAB
ATTN_BLOCK = 128
MXU_TILE = 256
HALO = 32

IN_TM = 512
OUT_TM = 512
CONV_SLABS = CONV_WIDTH // LANES
CONV_ROWS = 128
NORM_ROWS = 64
OUT_BLOCKS = 2
MIX_ROWS = 256
VMEM_LIMIT = 60 * 1024 * 1024

_OFF_VAL, _OFF_GLU, _OFF_AGATE = 0, CONV_WIDTH, 2 * CONV_WIDTH
_OFF_Q = 3 * CONV_WIDTH
_OFF_K, _OFF_V, _OFF_BGATE = _OFF_Q + ATTN_WIDTH, _OFF_Q + 2 * ATTN_WIDTH, _OFF_Q + 3 * ATTN_WIDTH


def _sigmoid(z):
    return 1.0 / (1.0 + jnp.exp(-z))


def _in_proj_kernel(x_ref, g_ref, w_ref, hsum_ref, qg_ref, kg_ref,
                    glu_ref, sga_ref, sgb_ref, q_ref, k_ref, v_ref, h_scr):
    x = x_ref[...]
    ms = jnp.mean(x * x, axis=-1, keepdims=True)
    h_scr[...] = (x * lax.rsqrt(ms + RMS_EPS) * g_ref[...]).astype(jnp.bfloat16)

    def proj(col):
        return jnp.dot(h_scr[...], w_ref[:, col:col + MXU_TILE],
                       preferred_element_type=jnp.float32)

    for c in range(CONV_WIDTH // MXU_TILE):
        lo = c * MXU_TILE
        sl = slice(lo, lo + MXU_TILE)
        glu_ref[:, sl] = (proj(_OFF_VAL + lo) * _sigmoid(proj(_OFF_GLU + lo))).astype(jnp.bfloat16)
        z = proj(_OFF_AGATE + lo)
        sga_ref[:, sl] = (z * _sigmoid(z)).astype(jnp.bfloat16)

    def head_norm(z, gain):
        ms = jnp.dot((z * z).astype(jnp.bfloat16), hsum_ref[...],
                     preferred_element_type=jnp.float32)
        return z * lax.rsqrt(ms + RMS_EPS) * gain

    def store_slabs(ref, c, val):
        for s in range(MXU_TILE // LANES):
            ref[c * (MXU_TILE // LANES) + s] = val[:, s * LANES:(s + 1) * LANES].astype(jnp.bfloat16)

    for c in range(ATTN_WIDTH // MXU_TILE):
        lo = c * MXU_TILE
        zq, zk = proj(_OFF_Q + lo), proj(_OFF_K + lo)
        nq, nk = head_norm(zq, qg_ref[...]), head_norm(zk, kg_ref[...])
        store_slabs(q_ref, c, nq)
        store_slabs(k_ref, c, nk)
        zv, zb = proj(_OFF_V + lo), proj(_OFF_BGATE + lo)
        store_slabs(v_ref, c, zv)
        sgb_ref[:, lo:lo + MXU_TILE] = (zb * _sigmoid(zb)).astype(jnp.bfloat16)


def _in_proj(x2, norm_g, w_in, hsum, qg, kg):
    n = x2.shape[0]
    const = lambda i: (0, 0)
    tok = lambda i: (i, 0)
    slab = lambda i: (0, i, 0)
    bf = jnp.bfloat16
    return pl.pallas_call(
        _in_proj_kernel,
        grid=(n // IN_TM,),
        in_specs=[
            pl.BlockSpec((IN_TM, D_MODEL), tok),
            pl.BlockSpec((1, D_MODEL), const),
            pl.BlockSpec((D_MODEL, IN_WIDTH), const, pipeline_mode=pl.Buffered(1)),
            pl.BlockSpec((MXU_TILE, MXU_TILE), const),
            pl.BlockSpec((1, MXU_TILE), const),
            pl.BlockSpec((1, MXU_TILE), const),
        ],
        out_specs=[
            pl.BlockSpec((IN_TM, CONV_WIDTH), tok),
            pl.BlockSpec((IN_TM, CONV_WIDTH), tok),
            pl.BlockSpec((IN_TM, ATTN_WIDTH), tok),
            pl.BlockSpec((N_SLABS, IN_TM, LANES), slab),
            pl.BlockSpec((N_SLABS, IN_TM, LANES), slab),
            pl.BlockSpec((N_SLABS, IN_TM, LANES), slab),
        ],
        out_shape=[
            jax.ShapeDtypeStruct((n, CONV_WIDTH), bf),
            jax.ShapeDtypeStruct((n, CONV_WIDTH), bf),
            jax.ShapeDtypeStruct((n, ATTN_WIDTH), bf),
            jax.ShapeDtypeStruct((N_SLABS, n, LANES), bf),
            jax.ShapeDtypeStruct((N_SLABS, n, LANES), bf),
            jax.ShapeDtypeStruct((N_SLABS, n, LANES), bf),
        ],
        scratch_shapes=[pltpu.VMEM((IN_TM, D_MODEL), bf)],
        compiler_params=pltpu.CompilerParams(
            dimension_semantics=("arbitrary",), vmem_limit_bytes=VMEM_LIMIT),
        name="in_proj",
    )(x2, norm_g, w_in, hsum, qg, kg)


def _score_stage(q_blk, k_blk, bias, first, mask_a, mask_b, row_max):
    q2 = jnp.concatenate([q_blk * mask_a, q_blk * mask_b], axis=0)
    s = lax.dot_general(q2, k_blk, (((1,), (1,)), ((), ())),
                        preferred_element_type=jnp.float32)
    s = s + jnp.concatenate([bias, bias], axis=0)
    if not row_max:
        return jnp.exp2(s).astype(jnp.bfloat16), None
    m = jnp.max(s, axis=-1, keepdims=True)
    p = jnp.exp2(s - m).astype(jnp.bfloat16)
    m_b = jnp.broadcast_to(m, (2 * ATTN_BLOCK, LANES))
    return p, jnp.where(first, m_b[:ATTN_BLOCK], m_b[ATTN_BLOCK:])


def _value_stage(p, vx_blk, first):
    pv = jnp.dot(p, vx_blk, preferred_element_type=jnp.float32)
    acc = jnp.where(first, pv[:ATTN_BLOCK, :LANES], pv[ATTN_BLOCK:, :LANES])
    den = jnp.where(first, pv[:ATTN_BLOCK, LANES:], pv[ATTN_BLOCK:, LANES:])
    return acc, den


def _attention_kernel(row_max, q_ref, k_ref, v_ref, sgb_ref, bias_ref, o_ref,
                      f32_scr, a4_scr, vx1_scr, q4_scr, k4_scr, vx4_scr, q16_scr, k16_scr, vx16_scr,
                      acc_scr, den_scr, *top_scr):
    seq = q_ref.shape[0]
    bf = jnp.bfloat16
    lane = lax.broadcasted_iota(jnp.int32, (ATTN_BLOCK, LANES), 1)
    first = lane < HEAD_DIM
    mask_a = first.astype(jnp.float32).astype(bf)
    mask_b = 1 - mask_a

    @pl.when((pl.program_id(0) == 0) & (pl.program_id(1) == 0))
    def _():
        vx1_scr[:, LANES:] = jnp.ones((seq, LANES), bf)
        vx4_scr[:, :, LANES:] = jnp.ones((4, seq // 4, LANES), bf)
        vx16_scr[:, :, LANES:] = jnp.ones((16, seq // 16, LANES), bf)

    vx1_scr[:, :LANES] = v_ref[...]

    for src, dst4, dst16 in ((q_ref, q4_scr, q16_scr), (k_ref, k4_scr, k16_scr), (v_ref, vx4_scr, vx16_scr)):
        f32_scr[...] = src[...].astype(jnp.float32)
        for r4 in range(4):
            part = f32_scr[pl.ds(r4, seq // 4, stride=4), :]
            a4_scr[r4] = part
            dst4[r4, :, :LANES] = part.astype(bf)
        for r4 in range(4):
            for hi in range(4):
                dst16[r4 + 4 * hi, :, :LANES] = a4_scr[r4, pl.ds(hi, seq // 16, stride=4), :].astype(bf)

    n_tiles = seq // ATTN_BLOCK

    patterns = (
        (1, lambda r, rows: q_ref[rows, :], lambda r, rows: k_ref[rows, :], lambda r, rows: vx1_scr[rows, :]),
        (4, lambda r, rows: q4_scr[r, rows, :], lambda r, rows: k4_scr[r, rows, :],
         lambda r, rows: vx4_scr[r, rows, :]),
        (16, lambda r, rows: q16_scr[r, rows, :], lambda r, rows: k16_scr[r, rows, :],
         lambda r, rows: vx16_scr[r, rows, :]),
    )

    for p_idx, (dil, q_src, k_src, vx_src) in enumerate(patterns):
        blk_per_res = n_tiles // dil
        for i in range(n_tiles):
            res, n = divmod(i, blk_per_res)
            row0 = n * ATTN_BLOCK
            k_rows = pl.ds(max(row0 - ATTN_BLOCK, 0), 2 * ATTN_BLOCK)
            p, top = _score_stage(q_src(res, pl.ds(row0, ATTN_BLOCK)), k_src(res, k_rows),
                                  bias_ref[min(n, 1)], first, mask_a, mask_b, row_max)
            acc, den = _value_stage(p, vx_src(res, k_rows), first)
            out = pl.ds(dil * row0 + res, ATTN_BLOCK, stride=dil) if dil > 1 else pl.ds(row0, ATTN_BLOCK)
            if row_max:
                top_scr[0][p_idx, out, :] = top
            acc_scr[p_idx, out, :] = acc
            den_scr[p_idx, out, :] = den

    def mix(i, carry):
        rows = pl.ds(pl.multiple_of(i * MIX_ROWS, MIX_ROWS), MIX_ROWS)
        if row_max:
            tops = [top_scr[0][p, rows, :] for p in range(3)]
            top = jnp.maximum(jnp.maximum(tops[0], tops[1]), tops[2])
            ws = [jnp.exp2(t - top) for t in tops]
            den = ws[0] * den_scr[0, rows, :] + ws[1] * den_scr[1, rows, :] + ws[2] * den_scr[2, rows, :]
            num = ws[0] * acc_scr[0, rows, :] + ws[1] * acc_scr[1, rows, :] + ws[2] * acc_scr[2, rows, :]
        else:
            den = den_scr[0, rows, :] + den_scr[1, rows, :] + den_scr[2, rows, :]
            num = acc_scr[0, rows, :] + acc_scr[1, rows, :] + acc_scr[2, rows, :]
        mixed = (num / den).astype(bf)
        o_ref[rows, :] = mixed * sgb_ref[rows, :]
        return carry

    lax.fori_loop(0, seq // MIX_ROWS, mix, 0)


def _attention(q, k, v, sgb, bias, batch, seq, row_max):
    bf = jnp.bfloat16
    slab = pl.BlockSpec((None, None, seq, LANES), lambda b, s: (s, b, 0, 0))
    tok = pl.BlockSpec((None, seq, LANES), lambda b, s: (b, 0, s))
    residue_major = lambda d, lanes: pltpu.VMEM((d, seq // d, lanes), bf)
    per_pattern = pltpu.VMEM((len(DILATED_PATTERNS), seq, LANES), jnp.float32)
    return pl.pallas_call(
        functools.partial(_attention_kernel, row_max),
        grid=(batch, N_SLABS),
        in_specs=[slab, slab, slab, tok,
                  pl.BlockSpec((2, ATTN_BLOCK, 2 * ATTN_BLOCK), lambda b, s: (0, 0, 0))],
        out_specs=tok,
        out_shape=jax.ShapeDtypeStruct((batch, seq, ATTN_WIDTH), bf),
        scratch_shapes=[
            pltpu.VMEM((seq, LANES), jnp.float32),
            pltpu.VMEM((4, seq // 4, LANES), jnp.float32),
            pltpu.VMEM((seq, 2 * LANES), bf),
            residue_major(4, LANES), residue_major(4, LANES), residue_major(4, 2 * LANES),
            residue_major(16, LANES), residue_major(16, LANES), residue_major(16, 2 * LANES),
        ] + [per_pattern] * (3 if row_max else 2),
        compiler_params=pltpu.CompilerParams(
            dimension_semantics=("arbitrary", "arbitrary"), vmem_limit_bytes=VMEM_LIMIT),
        name="dilated_attention_rowmax" if row_max else "dilated_attention",
    )(q.reshape(N_SLABS, batch, seq, LANES), k.reshape(N_SLABS, batch, seq, LANES),
      v.reshape(N_SLABS, batch, seq, LANES), sgb.reshape(batch, seq, ATTN_WIDTH), bias)


def _out_proj_kernel(tiles_per_seq, glu_ref, halo_ref, sga_ref, yb_ref, x_ref,
                     cw_ref, cb_ref, lng_ref, lnb_ref, pww_ref, pwb_ref, wout_ref,
                     y_ref, u_scr, c_scr, a_scr):
    tm = glu_ref.shape[0]
    first_tile = pl.program_id(0) % tiles_per_seq == 0
    for c in range(CONV_SLABS):
        lanes = slice(c * LANES, (c + 1) * LANES)
        halo = halo_ref[:, lanes].astype(jnp.float32)
        u_scr[c, 0:HALO, :] = jnp.where(first_tile, jnp.zeros_like(halo), halo)
        u_scr[c, HALO:HALO + tm, :] = glu_ref[:, lanes].astype(jnp.float32)

    def conv_chunk(c, row0):
        base = row0 + HALO - (CONV_KERNEL - 1)
        acc = jnp.broadcast_to(cb_ref[c], (CONV_ROWS, LANES))
        for t in range(CONV_KERNEL):
            acc = acc + u_scr[c, base + t:base + t + CONV_ROWS, :] * cw_ref[c, t:t + 1, :]
        c_scr[c, row0:row0 + CONV_ROWS, :] = acc

    def norm_chunk(row0):
        rows = slice(row0, row0 + NORM_ROWS)
        parts = [c_scr[c, rows, :] for c in range(CONV_SLABS)]
        mu = jnp.sum(functools.reduce(jnp.add, parts), axis=-1, keepdims=True) * (1.0 / CONV_WIDTH)
        cens = [p - mu for p in parts]
        var = jnp.sum(functools.reduce(jnp.add, [d * d for d in cens]), axis=-1, keepdims=True) * (1.0 / CONV_WIDTH)
        inv = lax.rsqrt(var + LN_EPS)
        for c in range(CONV_SLABS):
            yn = cens[c] * inv * lng_ref[c] + lnb_ref[c]
            a_scr[rows, c * LANES:(c + 1) * LANES] = (yn * _sigmoid(yn)).astype(jnp.bfloat16)

    block = tm // OUT_BLOCKS
    for blk in range(OUT_BLOCKS):
        rows = slice(blk * block, (blk + 1) * block)
        for c in range(CONV_SLABS):
            for row0 in range(rows.start, rows.stop, CONV_ROWS):
                conv_chunk(c, row0)
        for row0 in range(rows.start, rows.stop, NORM_ROWS):
            norm_chunk(row0)
        pw = jnp.dot(a_scr[rows, :], pww_ref[...], preferred_element_type=jnp.float32) + pwb_ref[...]
        y_a = (pw * sga_ref[rows, :].astype(jnp.float32)).astype(jnp.bfloat16)
        y = jnp.dot(y_a, wout_ref[0:CONV_WIDTH, :], preferred_element_type=jnp.float32)
        y = y + jnp.dot(yb_ref[rows, :], wout_ref[CONV_WIDTH:, :], preferred_element_type=jnp.float32)
        y_ref[rows, :] = x_ref[rows, :] + y


def _out_proj(glu, sga, yb, x2, cw, cb, lng, lnb, pww, pwb, wout, seq):
    n = x2.shape[0]
    tiles_per_seq = seq // OUT_TM
    halo_blocks = OUT_TM // HALO
    const = lambda i: (0, 0)
    const3 = lambda i: (0, 0, 0)
    tok = lambda i: (i, 0)
    resident = functools.partial(pl.BlockSpec, index_map=const, pipeline_mode=pl.Buffered(1))
    slab_rows = lambda rows: pl.BlockSpec((CONV_SLABS, rows, LANES), const3)
    return pl.pallas_call(
        functools.partial(_out_proj_kernel, tiles_per_seq),
        grid=(n // OUT_TM,),
        in_specs=[
            pl.BlockSpec((OUT_TM, CONV_WIDTH), tok),
            pl.BlockSpec((HALO, CONV_WIDTH), lambda i: (jnp.maximum(i * halo_blocks - 1, 0), 0)),
            pl.BlockSpec((OUT_TM, CONV_WIDTH), tok),
            pl.BlockSpec((OUT_TM, ATTN_WIDTH), tok),
            pl.BlockSpec((OUT_TM, D_MODEL), tok),
            slab_rows(CONV_KERNEL),
            slab_rows(1),
            slab_rows(1),
            slab_rows(1),
            resident((CONV_WIDTH, CONV_WIDTH)),
            pl.BlockSpec((1, CONV_WIDTH), const),
            resident((D_MODEL, D_MODEL)),
        ],
        out_specs=pl.BlockSpec((OUT_TM, D_MODEL), tok),
        out_shape=jax.ShapeDtypeStruct((n, D_MODEL), jnp.float32),
        scratch_shapes=[
            pltpu.VMEM((CONV_SLABS, HALO + OUT_TM, LANES), jnp.float32),
            pltpu.VMEM((CONV_SLABS, OUT_TM, LANES), jnp.float32),
            pltpu.VMEM((OUT_TM, CONV_WIDTH), jnp.bfloat16),
        ],
        compiler_params=pltpu.CompilerParams(
            dimension_semantics=("arbitrary",), vmem_limit_bytes=VMEM_LIMIT),
        name="out_proj",
    )(glu, glu, sga, yb, x2, cw, cb, lng, lnb, pww, pwb, wout)


def _attention_bias():
    qi = jnp.arange(ATTN_BLOCK)[:, None]
    ki = jnp.arange(2 * ATTN_BLOCK)[None, :]
    delta = ATTN_BLOCK + qi - ki
    later = (delta >= 0) & (delta <= ATTN_BLOCK)
    first = ki <= qi
    return jnp.where(jnp.stack([first, later]), 0.0, MASK_VALUE).astype(jnp.float32)


def _head_sum_matrix():
    head = jnp.arange(MXU_TILE) // HEAD_DIM
    return jnp.where(head[:, None] == head[None, :], 1.0 / HEAD_DIM, 0.0).astype(jnp.bfloat16)


def _layer(x, norm_g, w_in, conv_w, conv_b, conv_norm_g, conv_norm_b,
           conv_pw_w, conv_pw_b, q_norm_g, k_norm_g, w_out):
    batch, seq, _ = x.shape
    assert all(w // d == ATTN_BLOCK for w, d in DILATED_PATTERNS)
    assert seq % (16 * ATTN_BLOCK) == 0 and seq % OUT_TM == 0 and (batch * seq) % IN_TM == 0
    bf = jnp.bfloat16
    row = lambda a: a.reshape(1, -1).astype(jnp.float32)
    x2 = x.reshape(batch * seq, D_MODEL)
    heads_per_tile = MXU_TILE // HEAD_DIM
    qg = row(jnp.tile(q_norm_g, heads_per_tile)) * (HEAD_DIM ** -0.5 * LOG2_E)
    kg = row(jnp.tile(k_norm_g, heads_per_tile))
    glu, sga, sgb, q, k, v = _in_proj(x2, row(norm_g), w_in.astype(bf), _head_sum_matrix(), qg, kg)
    score_bound = (SCORE_BOUND_SLACK * HEAD_DIM ** 0.5 * LOG2_E
                   * jnp.max(jnp.abs(q_norm_g)) * jnp.max(jnp.abs(k_norm_g))).astype(jnp.float32)
    bias = _attention_bias()
    yb = lax.cond(
        score_bound <= MAX_SHARED_SHIFT,
        lambda: _attention(q, k, v, sgb, bias - score_bound, batch, seq, row_max=False),
        lambda: _attention(q, k, v, sgb, bias, batch, seq, row_max=True))
    slabs = lambda a: a.astype(jnp.float32).reshape(-1, CONV_SLABS, LANES).transpose(1, 0, 2)
    y = _out_proj(glu, sga, yb.reshape(batch * seq, ATTN_WIDTH), x2,
                  slabs(conv_w), slabs(conv_b), slabs(conv_norm_g), slabs(conv_norm_b),
                  conv_pw_w.astype(bf), row(conv_pw_b), w_out.astype(bf), seq)
    return y.reshape(batch, seq, D_MODEL)


def kernel(x, norm_g, w_in, conv_w, conv_b, conv_norm_g, conv_norm_b, conv_pw_w, conv_pw_b,
           q_norm_g, k_norm_g, w_out):
    for i in range(norm_g.shape[0]):
        x = _layer(x, norm_g[i], w_in[i], conv_w[i], conv_b[i], conv_norm_g[i], conv_norm_b[i],
                   conv_pw_w[i], conv_pw_b[i], q_norm_g[i], k_norm_g[i], w_out[i])
    return x
```

```python
import functools

import jax
import jax.numpy as jnp
from jax import lax
from jax.experimental import pallas as pl
from jax.experimental.pallas import tpu as pltpu

D_MODEL = 2048
CONV_WIDTH = 1024
CONV_KERNEL = 31
ATTN_WIDTH = 1024
N_HEADS = 16
HEAD_DIM = 64
IN_WIDTH = 3 * CONV_WIDTH + 4 * ATTN_WIDTH
DILATED_PATTERNS = ((128, 1), (512, 4), (2048, 16))
RMS_EPS = 1e-6
LN_EPS = 1e-5
MASK_VALUE = -1e30
LOG2_E = 1.4426950408889634
SCORE_BOUND_SLACK = 1.02
MAX_SHARED_SHIFT = 40.0

LANES = 128
HEADS_PER_SLAB = LANES // HEAD_DIM
N_SLABS = N_HEADS // HEADS_PER_SLAB
ATTN_BLOCK = 128
MXU_TILE = 256
HALO = 32

IN_TM = 512
OUT_TM = 512
CONV_SLABS = CONV_WIDTH // LANES
CONV_ROWS = 128
NORM_ROWS = 64
MIX_ROWS = 256
VMEM_LIMIT = 60 * 1024 * 1024

_OFF_VAL, _OFF_GLU, _OFF_AGATE = 0, CONV_WIDTH, 2 * CONV_WIDTH
_OFF_Q = 3 * CONV_WIDTH
_OFF_K, _OFF_V, _OFF_BGATE = _OFF_Q + ATTN_WIDTH, _OFF_Q + 2 * ATTN_WIDTH, _OFF_Q + 3 * ATTN_WIDTH


def _sigmoid(z):
    return 1.0 / (1.0 + jnp.exp(-z))


def _in_proj_kernel(tiles_per_seq, x_ref, g_ref, w_ref, hsum_ref, qg_ref, kg_ref,
                    cw_ref, cb_ref, lng_ref, lnb_ref, pww_ref, pwb_ref,
                    ya_ref, sgb_ref, q_ref, k_ref, v_ref,
                    h_scr, u_scr, c_scr, a_scr, sga_scr):
    tm = x_ref.shape[0]
    x = x_ref[...]
    ms = jnp.mean(x * x, axis=-1, keepdims=True)
    h_scr[...] = (x * lax.rsqrt(ms + RMS_EPS) * g_ref[...]).astype(jnp.bfloat16)

    @pl.when(pl.program_id(0) % tiles_per_seq == 0)
    def _():
        u_scr[:, 0:HALO, :] = jnp.zeros((CONV_SLABS, HALO, LANES), jnp.float32)

    def proj(col):
        return jnp.dot(h_scr[...], w_ref[:, col:col + MXU_TILE],
                       preferred_element_type=jnp.float32)

    def conv_chunk(c, row0):
        base = row0 + HALO - (CONV_KERNEL - 1)
        acc = jnp.broadcast_to(cb_ref[c], (CONV_ROWS, LANES))
        for t in range(CONV_KERNEL):
            acc = acc + u_scr[c, base + t:base + t + CONV_ROWS, :] * cw_ref[c, t:t + 1, :]
        c_scr[c, row0:row0 + CONV_ROWS, :] = acc

    def norm_chunk(row0):
        rows = slice(row0, row0 + NORM_ROWS)
        parts = [c_scr[c, rows, :] for c in range(CONV_SLABS)]
        mu = jnp.sum(functools.reduce(jnp.add, parts), axis=-1, keepdims=True) * (1.0 / CONV_WIDTH)
        cens = [p - mu for p in parts]
        var = jnp.sum(functools.reduce(jnp.add, [d * d for d in cens]), axis=-1, keepdims=True) * (1.0 / CONV_WIDTH)
        inv = lax.rsqrt(var + LN_EPS)
        for c in range(CONV_SLABS):
            yn = cens[c] * inv * lng_ref[c] + lnb_ref[c]
            a_scr[rows, c * LANES:(c + 1) * LANES] = (yn * _sigmoid(yn)).astype(jnp.bfloat16)

    slabs_per_tile = MXU_TILE // LANES
    for c in range(CONV_WIDTH // MXU_TILE):
        lo = c * MXU_TILE
        u = proj(_OFF_VAL + lo) * _sigmoid(proj(_OFF_GLU + lo))
        for s in range(slabs_per_tile):
            u_scr[c * slabs_per_tile + s, HALO:HALO + tm, :] = u[:, s * LANES:(s + 1) * LANES]
        z = proj(_OFF_AGATE + lo)
        sga_scr[:, lo:lo + MXU_TILE] = (z * _sigmoid(z)).astype(jnp.bfloat16)

    def head_norm(z, gain):
        ms = jnp.dot((z * z).astype(jnp.bfloat16), hsum_ref[...],
                     preferred_element_type=jnp.float32)
        return z * lax.rsqrt(ms + RMS_EPS) * gain

    def store_slabs(ref, c, val):
        for s in range(slabs_per_tile):
            ref[c * slabs_per_tile + s] = val[:, s * LANES:(s + 1) * LANES].astype(jnp.bfloat16)

    def conv_slabs(first):
        for c in range(first, first + slabs_per_tile):
            for row0 in range(0, tm, CONV_ROWS):
                conv_chunk(c, row0)

    def norm_rows(first, stop):
        for row0 in range(first, stop, NORM_ROWS):
            norm_chunk(row0)

    def pointwise(first):
        for c in range(first, first + 2):
            cols = slice(c * MXU_TILE, (c + 1) * MXU_TILE)
            pw = jnp.dot(a_scr[...], pww_ref[:, cols], preferred_element_type=jnp.float32) + pwb_ref[:, cols]
            ya_ref[:, cols] = (pw * sga_scr[:, cols].astype(jnp.float32)).astype(jnp.bfloat16)

    mixer_a = [functools.partial(conv_slabs, c) for c in range(0, CONV_SLABS, slabs_per_tile)]
    mixer_a += [functools.partial(norm_rows, 0, tm // 2), functools.partial(norm_rows, tm // 2, tm)]
    mixer_a += [functools.partial(pointwise, 0), functools.partial(pointwise, 2)]
    for c in range(ATTN_WIDTH // MXU_TILE):
        lo = c * MXU_TILE
        zq, zk = proj(_OFF_Q + lo), proj(_OFF_K + lo)
        nq, nk = head_norm(zq, qg_ref[...]), head_norm(zk, kg_ref[...])
        store_slabs(q_ref, c, nq)
        store_slabs(k_ref, c, nk)
        mixer_a[2 * c]()
        zv, zb = proj(_OFF_V + lo), proj(_OFF_BGATE + lo)
        store_slabs(v_ref, c, zv)
        sgb_ref[:, lo:lo + MXU_TILE] = (zb * _sigmoid(zb)).astype(jnp.bfloat16)
        mixer_a[2 * c + 1]()

    u_scr[:, 0:HALO, :] = u_scr[:, tm:tm + HALO, :]


def _in_proj(x2, norm_g, w_in, hsum, qg, kg, cw, cb, lng, lnb, pww, pwb, seq):
    n = x2.shape[0]
    const = lambda i: (0, 0)
    const3 = lambda i: (0, 0, 0)
    tok = lambda i: (i, 0)
    slab = lambda i: (0, i, 0)
    bf = jnp.bfloat16
    resident = functools.partial(pl.BlockSpec, index_map=const, pipeline_mode=pl.Buffered(1))
    slab_rows = lambda rows: pl.BlockSpec((CONV_SLABS, rows, LANES), const3)
    return pl.pallas_call(
        functools.partial(_in_proj_kernel, seq // IN_TM),
        grid=(n // IN_TM,),
        in_specs=[
            pl.BlockSpec((IN_TM, D_MODEL), tok),
            pl.BlockSpec((1, D_MODEL), const),
            resident((D_MODEL, IN_WIDTH)),
            pl.BlockSpec((MXU_TILE, MXU_TILE), const),
            pl.BlockSpec((1, MXU_TILE), const),
            pl.BlockSpec((1, MXU_TILE), const),
            slab_rows(CONV_KERNEL),
            slab_rows(1),
            slab_rows(1),
            slab_rows(1),
            resident((CONV_WIDTH, CONV_WIDTH)),
            pl.BlockSpec((1, CONV_WIDTH), const),
        ],
        out_specs=[
            pl.BlockSpec((IN_TM, CONV_WIDTH), tok),
            pl.BlockSpec((IN_TM, ATTN_WIDTH), tok),
            pl.BlockSpec((N_SLABS, IN_TM, LANES), slab),
            pl.BlockSpec((N_SLABS, IN_TM, LANES), slab),
            pl.BlockSpec((N_SLABS, IN_TM, LANES), slab),
        ],
        out_shape=[
            jax.ShapeDtypeStruct((n, CONV_WIDTH), bf),
            jax.ShapeDtypeStruct((n, ATTN_WIDTH), bf),
            jax.ShapeDtypeStruct((N_SLABS, n, LANES), bf),
            jax.ShapeDtypeStruct((N_SLABS, n, LANES), bf),
            jax.ShapeDtypeStruct((N_SLABS, n, LANES), bf),
        ],
        scratch_shapes=[
            pltpu.VMEM((IN_TM, D_MODEL), bf),
            pltpu.VMEM((CONV_SLABS, HALO + IN_TM, LANES), jnp.float32),
            pltpu.VMEM((CONV_SLABS, IN_TM, LANES), jnp.float32),
            pltpu.VMEM((IN_TM, CONV_WIDTH), bf),
            pltpu.VMEM((IN_TM, CONV_WIDTH), bf),
        ],
        compiler_params=pltpu.CompilerParams(
            dimension_semantics=("arbitrary",), vmem_limit_bytes=VMEM_LIMIT),
        name="in_proj",
    )(x2, norm_g, w_in, hsum, qg, kg, cw, cb, lng, lnb, pww, pwb)


def _score_stage(q_blk, k_blk, bias, first, mask_a, mask_b, row_max):
    q2 = jnp.concatenate([q_blk * mask_a, q_blk * mask_b], axis=0)
    s = lax.dot_general(q2, k_blk, (((1,), (1,)), ((), ())),
                        preferred_element_type=jnp.float32)
    s = s + jnp.concatenate([bias, bias], axis=0)
    if not row_max:
        return jnp.exp2(s).astype(jnp.bfloat16), None
    m = jnp.max(s, axis=-1, keepdims=True)
    p = jnp.exp2(s - m).astype(jnp.bfloat16)
    m_b = jnp.broadcast_to(m, (2 * ATTN_BLOCK, LANES))
    return p, jnp.where(first, m_b[:ATTN_BLOCK], m_b[ATTN_BLOCK:])


def _value_stage(p, vx_blk, first):
    pv = jnp.dot(p, vx_blk, preferred_element_type=jnp.float32)
    acc = jnp.where(first, pv[:ATTN_BLOCK, :LANES], pv[ATTN_BLOCK:, :LANES])
    den = jnp.where(first, pv[:ATTN_BLOCK, LANES:], pv[ATTN_BLOCK:, LANES:])
    return acc, den


def _attention_kernel(row_max, q_ref, k_ref, v_ref, sgb_ref, bias_ref, o_ref,
                      f32_scr, a4_scr, vx1_scr, q4_scr, k4_scr, vx4_scr, q16_scr, k16_scr, vx16_scr,
                      acc_scr, den_scr, *top_scr):
    seq = q_ref.shape[0]
    bf = jnp.bfloat16
    lane = lax.broadcasted_iota(jnp.int32, (ATTN_BLOCK, LANES), 1)
    first = lane < HEAD_DIM
    mask_a = first.astype(jnp.float32).astype(bf)
    mask_b = 1 - mask_a

    @pl.when((pl.program_id(0) == 0) & (pl.program_id(1) == 0))
    def _():
        vx1_scr[:, LANES:] = jnp.ones((seq, LANES), bf)
        vx4_scr[:, :, LANES:] = jnp.ones((4, seq // 4, LANES), bf)
        vx16_scr[:, :, LANES:] = jnp.ones((16, seq // 16, LANES), bf)

    vx1_scr[:, :LANES] = v_ref[...]

    for src, dst4, dst16 in ((q_ref, q4_scr, q16_scr), (k_ref, k4_scr, k16_scr), (v_ref, vx4_scr, vx16_scr)):
        f32_scr[...] = src[...].astype(jnp.float32)
        for r4 in range(4):
            part = f32_scr[pl.ds(r4, seq // 4, stride=4), :]
            a4_scr[r4] = part
            dst4[r4, :, :LANES] = part.astype(bf)
        for r4 in range(4):
            for hi in range(4):
                dst16[r4 + 4 * hi, :, :LANES] = a4_scr[r4, pl.ds(hi, seq // 16, stride=4), :].astype(bf)

    n_tiles = seq // ATTN_BLOCK

    patterns = (
        (1, lambda r, rows: q_ref[rows, :], lambda r, rows: k_ref[rows, :], lambda r, rows: vx1_scr[rows, :]),
        (4, lambda r, rows: q4_scr[r, rows, :], lambda r, rows: k4_scr[r, rows, :],
         lambda r, rows: vx4_scr[r, rows, :]),
        (16, lambda r, rows: q16_scr[r, rows, :], lambda r, rows: k16_scr[r, rows, :],
         lambda r, rows: vx16_scr[r, rows, :]),
    )

    for p_idx, (dil, q_src, k_src, vx_src) in enumerate(patterns):
        blk_per_res = n_tiles // dil
        for i in range(n_tiles):
            res, n = divmod(i, blk_per_res)
            row0 = n * ATTN_BLOCK
            k_rows = pl.ds(max(row0 - ATTN_BLOCK, 0), 2 * ATTN_BLOCK)
            p, top = _score_stage(q_src(res, pl.ds(row0, ATTN_BLOCK)), k_src(res, k_rows),
                                  bias_ref[min(n, 1)], first, mask_a, mask_b, row_max)
            acc, den = _value_stage(p, vx_src(res, k_rows), first)
            out = pl.ds(dil * row0 + res, ATTN_BLOCK, stride=dil) if dil > 1 else pl.ds(row0, ATTN_BLOCK)
            if row_max:
                top_scr[0][p_idx, out, :] = top
            acc_scr[p_idx, out, :] = acc
            den_scr[p_idx, out, :] = den

    def mix(i, carry):
        rows = pl.ds(pl.multiple_of(i * MIX_ROWS, MIX_ROWS), MIX_ROWS)
        if row_max:
            tops = [top_scr[0][p, rows, :] for p in range(3)]
            top = jnp.maximum(jnp.maximum(tops[0], tops[1]), tops[2])
            ws = [jnp.exp2(t - top) for t in tops]
            den = ws[0] * den_scr[0, rows, :] + ws[1] * den_scr[1, rows, :] + ws[2] * den_scr[2, rows, :]
            num = ws[0] * acc_scr[0, rows, :] + ws[1] * acc_scr[1, rows, :] + ws[2] * acc_scr[2, rows, :]
        else:
            den = den_scr[0, rows, :] + den_scr[1, rows, :] + den_scr[2, rows, :]
            num = acc_scr[0, rows, :] + acc_scr[1, rows, :] + acc_scr[2, rows, :]
        mixed = (num / den).astype(bf)
        o_ref[rows, :] = mixed * sgb_ref[rows, :]
        return carry

    lax.fori_loop(0, seq // MIX_ROWS, mix, 0)


def _attention(q, k, v, sgb, bias, batch, seq, row_max):
    bf = jnp.bfloat16
    slab = pl.BlockSpec((None, None, seq, LANES), lambda b, s: (s, b, 0, 0))
    tok = pl.BlockSpec((None, seq, LANES), lambda b, s: (b, 0, s))
    residue_major = lambda d, lanes: pltpu.VMEM((d, seq // d, lanes), bf)
    per_pattern = pltpu.VMEM((len(DILATED_PATTERNS), seq, LANES), jnp.float32)
    return pl.pallas_call(
        functools.partial(_attention_kernel, row_max),
        grid=(batch, N_SLABS),
        in_specs=[slab, slab, slab, tok,
                  pl.BlockSpec((2, ATTN_BLOCK, 2 * ATTN_BLOCK), lambda b, s: (0, 0, 0))],
        out_specs=tok,
        out_shape=jax.ShapeDtypeStruct((batch, seq, ATTN_WIDTH), bf),
        scratch_shapes=[
            pltpu.VMEM((seq, LANES), jnp.float32),
            pltpu.VMEM((4, seq // 4, LANES), jnp.float32),
            pltpu.VMEM((seq, 2 * LANES), bf),
            residue_major(4, LANES), residue_major(4, LANES), residue_major(4, 2 * LANES),
            residue_major(16, LANES), residue_major(16, LANES), residue_major(16, 2 * LANES),
        ] + [per_pattern] * (3 if row_max else 2),
        compiler_params=pltpu.CompilerParams(
            dimension_semantics=("arbitrary", "arbitrary"), vmem_limit_bytes=VMEM_LIMIT),
        name="dilated_attention_rowmax" if row_max else "dilated_attention",
    )(q.reshape(N_SLABS, batch, seq, LANES), k.reshape(N_SLABS, batch, seq, LANES),
      v.reshape(N_SLABS, batch, seq, LANES), sgb.reshape(batch, seq, ATTN_WIDTH), bias)


def _out_proj_kernel(ya_ref, yb_ref, x_ref, wout_ref, y_ref):
    y = jnp.dot(ya_ref[...], wout_ref[0:CONV_WIDTH, :], preferred_element_type=jnp.float32)
    y = y + jnp.dot(yb_ref[...], wout_ref[CONV_WIDTH:, :], preferred_element_type=jnp.float32)
    y_ref[...] = x_ref[...] + y


def _out_proj(ya, yb, x2, wout):
    n = x2.shape[0]
    tok = lambda i: (i, 0)
    return pl.pallas_call(
        _out_proj_kernel,
        grid=(n // OUT_TM,),
        in_specs=[
            pl.BlockSpec((OUT_TM, CONV_WIDTH), tok),
            pl.BlockSpec((OUT_TM, ATTN_WIDTH), tok),
            pl.BlockSpec((OUT_TM, D_MODEL), tok),
            pl.BlockSpec((D_MODEL, D_MODEL), lambda i: (0, 0), pipeline_mode=pl.Buffered(1)),
        ],
        out_specs=pl.BlockSpec((OUT_TM, D_MODEL), tok),
        out_shape=jax.ShapeDtypeStruct((n, D_MODEL), jnp.float32),
        compiler_params=pltpu.CompilerParams(
            dimension_semantics=("arbitrary",), vmem_limit_bytes=VMEM_LIMIT),
        name="out_proj",
    )(ya, yb, x2, wout)


def _attention_bias():
    qi = jnp.arange(ATTN_BLOCK)[:, None]
    ki = jnp.arange(2 * ATTN_BLOCK)[None, :]
    delta = ATTN_BLOCK + qi - ki
    later = (delta >= 0) & (delta <= ATTN_BLOCK)
    first = ki <= qi
    return jnp.where(jnp.stack([first, later]), 0.0, MASK_VALUE).astype(jnp.float32)


def _head_sum_matrix():
    head = jnp.arange(MXU_TILE) // HEAD_DIM
    return jnp.where(head[:, None] == head[None, :], 1.0 / HEAD_DIM, 0.0).astype(jnp.bfloat16)


def _layer(x, norm_g, w_in, conv_w, conv_b, conv_norm_g, conv_norm_b,
           conv_pw_w, conv_pw_b, q_norm_g, k_norm_g, w_out):
    batch, seq, _ = x.shape
    assert all(w // d == ATTN_BLOCK for w, d in DILATED_PATTERNS)
    assert seq % (16 * ATTN_BLOCK) == 0 and seq % IN_TM == 0 and (batch * seq) % OUT_TM == 0
    bf = jnp.bfloat16
    row = lambda a: a.reshape(1, -1).astype(jnp.float32)
    x2 = x.reshape(batch * seq, D_MODEL)
    heads_per_tile = MXU_TILE // HEAD_DIM
    qg = row(jnp.tile(q_norm_g, heads_per_tile)) * (HEAD_DIM ** -0.5 * LOG2_E)
    kg = row(jnp.tile(k_norm_g, heads_per_tile))
    slabs = lambda a: a.astype(jnp.float32).reshape(-1, CONV_SLABS, LANES).transpose(1, 0, 2)
    ya, sgb, q, k, v = _in_proj(x2, row(norm_g), w_in.astype(bf), _head_sum_matrix(), qg, kg,
                                slabs(conv_w), slabs(conv_b), slabs(conv_norm_g), slabs(conv_norm_b),
                                conv_pw_w.astype(bf), row(conv_pw_b), seq)
    score_bound = (SCORE_BOUND_SLACK * HEAD_DIM ** 0.5 * LOG2_E
                   * jnp.max(jnp.abs(q_norm_g)) * jnp.max(jnp.abs(k_norm_g))).astype(jnp.float32)
    bias = _attention_bias()
    yb = lax.cond(
        score_bound <= MAX_SHARED_SHIFT,
        lambda: _attention(q, k, v, sgb, bias - score_bound, batch, seq, row_max=False),
        lambda: _attention(q, k, v, sgb, bias, batch, seq, row_max=True))
    y = _out_proj(ya, yb.reshape(batch * seq, ATTN_WIDTH), x2, w_out.astype(bf))
    return y.reshape(batch, seq, D_MODEL)


def kernel(x, norm_g, w_in, conv_w, conv_b, conv_norm_g, conv_norm_b, conv_pw_w, conv_pw_b,
           q_norm_g, k_norm_g, w_out):
    for i in range(norm_g.shape[0]):
        x = _layer(x, norm_g[i], w_in[i], conv_w[i], conv_b[i], conv_norm_g[i], conv_norm_b[i],
                   conv_pw_w[i], conv_pw_b[i], q_norm_g[i], k_norm_g[i], w_out[i])
    return x
```

```python
import functools

import jax
import jax.numpy as jnp
from jax import lax
from jax.experimental import pallas as pl
from jax.experimental.pallas import tpu as pltpu

D_MODEL = 2048
CONV_WIDTH = 1024
CONV_KERNEL = 31
ATTN_WIDTH = 1024
N_HEADS = 16
HEAD_DIM = 64
IN_WIDTH = 3 * CONV_WIDTH + 4 * ATTN_WIDTH
DILATED_PATTERNS = ((128, 1), (512, 4), (2048, 16))
RMS_EPS = 1e-6
LN_EPS = 1e-5
MASK_VALUE = -1e30
LOG2_E = 1.4426950408889634
SCORE_BOUND_SLACK = 1.02
MAX_SHARED_SHIFT = 40.0

LANES = 128
HEADS_PER_SLAB = LANES // HEAD_DIM
N_SLABS = N_HEADS // HEADS_PER_SLAB
ATTN_BLOCK = 128
MXU_TILE = 256
HALO = 32

IN_TM = 512
OUT_TM = 512
CONV_SLABS = CONV_WIDTH // LANES
CONV_ROWS = 128
NORM_ROWS = 64
MIX_ROWS = 256
VMEM_LIMIT = 60 * 1024 * 1024

_OFF_VAL, _OFF_GLU, _OFF_AGATE = 0, CONV_WIDTH, 2 * CONV_WIDTH
_OFF_Q = 3 * CONV_WIDTH
_OFF_K, _OFF_V, _OFF_BGATE = _OFF_Q + ATTN_WIDTH, _OFF_Q + 2 * ATTN_WIDTH, _OFF_Q + 3 * ATTN_WIDTH


def _sigmoid(z):
    return 1.0 / (1.0 + jnp.exp(-z))


def _in_proj_kernel(tiles_per_seq, x_ref, g_ref, w_ref, hsum_ref, qg_ref, kg_ref,
                    cw_ref, cb_ref, lng_ref, lnb_ref, pww_ref, pwb_ref,
                    ya_ref, sgb_ref, q_ref, k_ref, v_ref,
                    h_scr, u_scr, c_scr, a_scr, sga_scr):
    tm = x_ref.shape[0]
    x = x_ref[...]
    ms = jnp.mean(x * x, axis=-1, keepdims=True)
    h_scr[...] = (x * lax.rsqrt(ms + RMS_EPS) * g_ref[...]).astype(jnp.bfloat16)

    @pl.when(pl.program_id(0) % tiles_per_seq == 0)
    def _():
        u_scr[:, 0:HALO, :] = jnp.zeros((CONV_SLABS, HALO, LANES), jnp.float32)

    def proj(col):
        return jnp.dot(h_scr[...], w_ref[:, col:col + MXU_TILE],
                       preferred_element_type=jnp.float32)

    def conv_chunk(c, row0):
        base = row0 + HALO - (CONV_KERNEL - 1)
        acc = jnp.broadcast_to(cb_ref[c], (CONV_ROWS, LANES))
        for t in range(CONV_KERNEL):
            acc = acc + u_scr[c, base + t:base + t + CONV_ROWS, :] * cw_ref[c, t:t + 1, :]
        c_scr[c, row0:row0 + CONV_ROWS, :] = acc

    def norm_chunk(row0):
        rows = slice(row0, row0 + NORM_ROWS)
        parts = [c_scr[c, rows, :] for c in range(CONV_SLABS)]
        mu = jnp.sum(functools.reduce(jnp.add, parts), axis=-1, keepdims=True) * (1.0 / CONV_WIDTH)
        cens = [p - mu for p in parts]
        var = jnp.sum(functools.reduce(jnp.add, [d * d for d in cens]), axis=-1, keepdims=True) * (1.0 / CONV_WIDTH)
        inv = lax.rsqrt(var + LN_EPS)
        for c in range(CONV_SLABS):
            yn = cens[c] * inv * lng_ref[c] + lnb_ref[c]
            a_scr[rows, c * LANES:(c + 1) * LANES] = (yn * _sigmoid(yn)).astype(jnp.bfloat16)

    slabs_per_tile = MXU_TILE // LANES
    for c in range(CONV_WIDTH // MXU_TILE):
        lo = c * MXU_TILE
        u = proj(_OFF_VAL + lo) * _sigmoid(proj(_OFF_GLU + lo))
        for s in range(slabs_per_tile):
            u_scr[c * slabs_per_tile + s, HALO:HALO + tm, :] = u[:, s * LANES:(s + 1) * LANES]
        z = proj(_OFF_AGATE + lo)
        sga_scr[:, lo:lo + MXU_TILE] = (z * _sigmoid(z)).astype(jnp.bfloat16)

    def head_norm(z, gain):
        ms = jnp.dot((z * z).astype(jnp.bfloat16), hsum_ref[...],
                     preferred_element_type=jnp.float32)
        return z * lax.rsqrt(ms + RMS_EPS) * gain

    def store_slabs(ref, c, val):
        for s in range(slabs_per_tile):
            ref[c * slabs_per_tile + s] = val[:, s * LANES:(s + 1) * LANES].astype(jnp.bfloat16)

    def conv_slabs(first):
        for c in range(first, first + slabs_per_tile):
            for row0 in range(0, tm, CONV_ROWS):
                conv_chunk(c, row0)

    def norm_rows(first, stop):
        for row0 in range(first, stop, NORM_ROWS):
            norm_chunk(row0)

    def pointwise(first):
        for c in range(first, first + 2):
            cols = slice(c * MXU_TILE, (c + 1) * MXU_TILE)
            pw = jnp.dot(a_scr[...], pww_ref[:, cols], preferred_element_type=jnp.float32) + pwb_ref[:, cols]
            ya_ref[:, cols] = (pw * sga_scr[:, cols].astype(jnp.float32)).astype(jnp.bfloat16)

    mixer_a = [functools.partial(conv_slabs, c) for c in range(0, CONV_SLABS, slabs_per_tile)]
    mixer_a += [functools.partial(norm_rows, 0, tm // 2), functools.partial(norm_rows, tm // 2, tm)]
    mixer_a += [functools.partial(pointwise, 0), functools.partial(pointwise, 2)]
    for c in range(ATTN_WIDTH // MXU_TILE):
        lo = c * MXU_TILE
        zq, zk = proj(_OFF_Q + lo), proj(_OFF_K + lo)
        nq, nk = head_norm(zq, qg_ref[...]), head_norm(zk, kg_ref[...])
        store_slabs(q_ref, c, nq)
        store_slabs(k_ref, c, nk)
        mixer_a[2 * c]()
        zv, zb = proj(_OFF_V + lo), proj(_OFF_BGATE + lo)
        store_slabs(v_ref, c, zv)
        sgb_ref[:, lo:lo + MXU_TILE] = (zb * _sigmoid(zb)).astype(jnp.bfloat16)
        mixer_a[2 * c + 1]()

    u_scr[:, 0:HALO, :] = u_scr[:, tm:tm + HALO, :]


def _in_proj(x2, norm_g, w_in, hsum, qg, kg, cw, cb, lng, lnb, pww, pwb, seq):
    n = x2.shape[0]
    const = lambda i: (0, 0)
    const3 = lambda i: (0, 0, 0)
    tok = lambda i: (i, 0)
    slab = lambda i: (0, i, 0)
    bf = jnp.bfloat16
    resident = functools.partial(pl.BlockSpec, index_map=const, pipeline_mode=pl.Buffered(1))
    slab_rows = lambda rows: pl.BlockSpec((CONV_SLABS, rows, LANES), const3)
    return pl.pallas_call(
        functools.partial(_in_proj_kernel, seq // IN_TM),
        grid=(n // IN_TM,),
        in_specs=[
            pl.BlockSpec((IN_TM, D_MODEL), tok),
            pl.BlockSpec((1, D_MODEL), const),
            resident((D_MODEL, IN_WIDTH)),
            pl.BlockSpec((MXU_TILE, MXU_TILE), const),
            pl.BlockSpec((1, MXU_TILE), const),
            pl.BlockSpec((1, MXU_TILE), const),
            slab_rows(CONV_KERNEL),
            slab_rows(1),
            slab_rows(1),
            slab_rows(1),
            resident((CONV_WIDTH, CONV_WIDTH)),
            pl.BlockSpec((1, CONV_WIDTH), const),
        ],
        out_specs=[
            pl.BlockSpec((IN_TM, CONV_WIDTH), tok),
            pl.BlockSpec((IN_TM, ATTN_WIDTH), tok),
            pl.BlockSpec((N_SLABS, IN_TM, LANES), slab),
            pl.BlockSpec((N_SLABS, IN_TM, LANES), slab),
            pl.BlockSpec((N_SLABS, IN_TM, LANES), slab),
        ],
        out_shape=[
            jax.ShapeDtypeStruct((n, CONV_WIDTH), bf),
            jax.ShapeDtypeStruct((n, ATTN_WIDTH), bf),
            jax.ShapeDtypeStruct((N_SLABS, n, LANES), bf),
            jax.ShapeDtypeStruct((N_SLABS, n, LANES), bf),
            jax.ShapeDtypeStruct((N_SLABS, n, LANES), bf),
        ],
        scratch_shapes=[
            pltpu.VMEM((IN_TM, D_MODEL), bf),
            pltpu.VMEM((CONV_SLABS, HALO + IN_TM, LANES), jnp.float32),
            pltpu.VMEM((CONV_SLABS, IN_TM, LANES), jnp.float32),
            pltpu.VMEM((IN_TM, CONV_WIDTH), bf),
            pltpu.VMEM((IN_TM, CONV_WIDTH), bf),
        ],
        compiler_params=pltpu.CompilerParams(
            dimension_semantics=("arbitrary",), vmem_limit_bytes=VMEM_LIMIT),
        name="in_proj",
    )(x2, norm_g, w_in, hsum, qg, kg, cw, cb, lng, lnb, pww, pwb)


def _score_stage(q_blk, k_blk, bias, first, mask_a, mask_b, row_max):
    q2 = jnp.concatenate([q_blk * mask_a, q_blk * mask_b], axis=0)
    s = lax.dot_general(q2, k_blk, (((1,), (1,)), ((), ())),
                        preferred_element_type=jnp.float32)
    s = s + jnp.concatenate([bias, bias], axis=0)
    if not row_max:
        return jnp.exp2(s).astype(jnp.bfloat16), None
    m = jnp.max(s, axis=-1, keepdims=True)
    p = jnp.exp2(s - m).astype(jnp.bfloat16)
    m_b = jnp.broadcast_to(m, (2 * ATTN_BLOCK, LANES))
    return p, jnp.where(first, m_b[:ATTN_BLOCK], m_b[ATTN_BLOCK:])


def _value_stage(p, vx_blk, first):
    pv = jnp.dot(p, vx_blk, preferred_element_type=jnp.float32)
    acc = jnp.where(first, pv[:ATTN_BLOCK, :LANES], pv[ATTN_BLOCK:, :LANES])
    den = jnp.where(first, pv[:ATTN_BLOCK, LANES:], pv[ATTN_BLOCK:, LANES:])
    return acc, den


def _attention_kernel(row_max, q_ref, k_ref, v_ref, sgb_ref, bias_ref, o_ref,
                      f32_scr, a4_scr, vx1_scr, q4_scr, k4_scr, vx4_scr, q16_scr, k16_scr, vx16_scr,
                      acc_scr, den_scr, aux_scr):
    seq = q_ref.shape[0]
    bf = jnp.bfloat16
    top_scr = mod4_scr = aux_scr
    lane = lax.broadcasted_iota(jnp.int32, (ATTN_BLOCK, LANES), 1)
    first = lane < HEAD_DIM
    mask_a = first.astype(jnp.float32).astype(bf)
    mask_b = 1 - mask_a

    @pl.when((pl.program_id(0) == 0) & (pl.program_id(1) == 0))
    def _():
        vx1_scr[:, LANES:] = jnp.ones((seq, LANES), bf)
        vx4_scr[:, :, LANES:] = jnp.ones((4, seq // 4, LANES), bf)
        vx16_scr[:, :, LANES:] = jnp.ones((16, seq // 16, LANES), bf)

    vx1_scr[:, :LANES] = v_ref[...]

    for src, dst4, dst16 in ((q_ref, q4_scr, q16_scr), (k_ref, k4_scr, k16_scr), (v_ref, vx4_scr, vx16_scr)):
        f32_scr[...] = src[...].astype(jnp.float32)
        for r4 in range(4):
            part = f32_scr[pl.ds(r4, seq // 4, stride=4), :]
            a4_scr[r4] = part
            dst4[r4, :, :LANES] = part.astype(bf)
        for r4 in range(4):
            for hi in range(4):
                dst16[r4 + 4 * hi, :, :LANES] = a4_scr[r4, pl.ds(hi, seq // 16, stride=4), :].astype(bf)

    n_tiles = seq // ATTN_BLOCK

    patterns = (
        (1, lambda r, rows: q_ref[rows, :], lambda r, rows: k_ref[rows, :], lambda r, rows: vx1_scr[rows, :]),
        (4, lambda r, rows: q4_scr[r, rows, :], lambda r, rows: k4_scr[r, rows, :],
         lambda r, rows: vx4_scr[r, rows, :]),
        (16, lambda r, rows: q16_scr[r, rows, :], lambda r, rows: k16_scr[r, rows, :],
         lambda r, rows: vx16_scr[r, rows, :]),
    )

    for p_idx, (dil, q_src, k_src, vx_src) in enumerate(patterns):
        blk_per_res = n_tiles // dil
        for i in range(n_tiles):
            res, n = divmod(i, blk_per_res)
            row0 = n * ATTN_BLOCK
            k_rows = pl.ds(max(row0 - ATTN_BLOCK, 0), 2 * ATTN_BLOCK)
            p, top = _score_stage(q_src(res, pl.ds(row0, ATTN_BLOCK)), k_src(res, k_rows),
                                  bias_ref[min(n, 1)], first, mask_a, mask_b, row_max)
            acc, den = _value_stage(p, vx_src(res, k_rows), first)
            if dil == 16 and not row_max:
                out = pl.ds(4 * row0 + res // 4, ATTN_BLOCK, stride=4)
                mod4_scr[0, res % 4, out, :] = acc
                mod4_scr[1, res % 4, out, :] = den
                continue
            out = pl.ds(dil * row0 + res, ATTN_BLOCK, stride=dil) if dil > 1 else pl.ds(row0, ATTN_BLOCK)
            if row_max:
                top_scr[p_idx, out, :] = top
            acc_scr[p_idx, out, :] = acc
            den_scr[p_idx, out, :] = den

    if not row_max:
        for r4 in range(4):
            rows = pl.ds(r4, seq // 4, stride=4)
            acc_scr[2, rows, :] = mod4_scr[0, r4]
            den_scr[2, rows, :] = mod4_scr[1, r4]

    def mix(i, carry):
        rows = pl.ds(pl.multiple_of(i * MIX_ROWS, MIX_ROWS), MIX_ROWS)
        if row_max:
            tops = [top_scr[p, rows, :] for p in range(3)]
            top = jnp.maximum(jnp.maximum(tops[0], tops[1]), tops[2])
            ws = [jnp.exp2(t - top) for t in tops]
            den = ws[0] * den_scr[0, rows, :] + ws[1] * den_scr[1, rows, :] + ws[2] * den_scr[2, rows, :]
            num = ws[0] * acc_scr[0, rows, :] + ws[1] * acc_scr[1, rows, :] + ws[2] * acc_scr[2, rows, :]
        else:
            den = den_scr[0, rows, :] + den_scr[1, rows, :] + den_scr[2, rows, :]
            num = acc_scr[0, rows, :] + acc_scr[1, rows, :] + acc_scr[2, rows, :]
        mixed = (num / den).astype(bf)
        o_ref[rows, :] = mixed * sgb_ref[rows, :]
        return carry

    lax.fori_loop(0, seq // MIX_ROWS, mix, 0)


def _attention(q, k, v, sgb, bias, batch, seq, row_max):
    bf = jnp.bfloat16
    slab = pl.BlockSpec((None, None, seq, LANES), lambda b, s: (s, b, 0, 0))
    tok = pl.BlockSpec((None, seq, LANES), lambda b, s: (b, 0, s))
    residue_major = lambda d, lanes: pltpu.VMEM((d, seq // d, lanes), bf)
    per_pattern = pltpu.VMEM((len(DILATED_PATTERNS), seq, LANES), jnp.float32)
    return pl.pallas_call(
        functools.partial(_attention_kernel, row_max),
        grid=(batch, N_SLABS),
        in_specs=[slab, slab, slab, tok,
                  pl.BlockSpec((2, ATTN_BLOCK, 2 * ATTN_BLOCK), lambda b, s: (0, 0, 0))],
        out_specs=tok,
        out_shape=jax.ShapeDtypeStruct((batch, seq, ATTN_WIDTH), bf),
        scratch_shapes=[
            pltpu.VMEM((seq, LANES), jnp.float32),
            pltpu.VMEM((4, seq // 4, LANES), jnp.float32),
            pltpu.VMEM((seq, 2 * LANES), bf),
            residue_major(4, LANES), residue_major(4, LANES), residue_major(4, 2 * LANES),
            residue_major(16, LANES), residue_major(16, LANES), residue_major(16, 2 * LANES),
        ] + [per_pattern] * 2 + [per_pattern if row_max else pltpu.VMEM((2, 4, seq // 4, LANES), jnp.float32)],
        compiler_params=pltpu.CompilerParams(
            dimension_semantics=("arbitrary", "arbitrary"), vmem_limit_bytes=VMEM_LIMIT),
        name="dilated_attention_rowmax" if row_max else "dilated_attention",
    )(q.reshape(N_SLABS, batch, seq, LANES), k.reshape(N_SLABS, batch, seq, LANES),
      v.reshape(N_SLABS, batch, seq, LANES), sgb.reshape(batch, seq, ATTN_WIDTH), bias)


def _out_proj_kernel(ya_ref, yb_ref, x_ref, wout_ref, y_ref):
    y = jnp.dot(ya_ref[...], wout_ref[0:CONV_WIDTH, :], preferred_element_type=jnp.float32)
    y = y + jnp.dot(yb_ref[...], wout_ref[CONV_WIDTH:, :], preferred_element_type=jnp.float32)
    y_ref[...] = x_ref[...] + y


def _out_proj(ya, yb, x2, wout):
    n = x2.shape[0]
    tok = lambda i: (i, 0)
    return pl.pallas_call(
        _out_proj_kernel,
        grid=(n // OUT_TM,),
        in_specs=[
            pl.BlockSpec((OUT_TM, CONV_WIDTH), tok),
            pl.BlockSpec((OUT_TM, ATTN_WIDTH), tok),
            pl.BlockSpec((OUT_TM, D_MODEL), tok),
            pl.BlockSpec((D_MODEL, D_MODEL), lambda i: (0, 0), pipeline_mode=pl.Buffered(1)),
        ],
        out_specs=pl.BlockSpec((OUT_TM, D_MODEL), tok),
        out_shape=jax.ShapeDtypeStruct((n, D_MODEL), jnp.float32),
        compiler_params=pltpu.CompilerParams(
            dimension_semantics=("arbitrary",), vmem_limit_bytes=VMEM_LIMIT),
        name="out_proj",
    )(ya, yb, x2, wout)


def _attention_bias():
    qi = jnp.arange(ATTN_BLOCK)[:, None]
    ki = jnp.arange(2 * ATTN_BLOCK)[None, :]
    delta = ATTN_BLOCK + qi - ki
    later = (delta >= 0) & (delta <= ATTN_BLOCK)
    first = ki <= qi
    return jnp.where(jnp.stack([first, later]), 0.0, MASK_VALUE).astype(jnp.float32)


def _head_sum_matrix():
    head = jnp.arange(MXU_TILE) // HEAD_DIM
    return jnp.where(head[:, None] == head[None, :], 1.0 / HEAD_DIM, 0.0).astype(jnp.bfloat16)


def _layer(x, norm_g, w_in, conv_w, conv_b, conv_norm_g, conv_norm_b,
           conv_pw_w, conv_pw_b, q_norm_g, k_norm_g, w_out):
    batch, seq, _ = x.shape
    assert all(w // d == ATTN_BLOCK for w, d in DILATED_PATTERNS)
    assert seq % (16 * ATTN_BLOCK) == 0 and seq % IN_TM == 0 and (batch * seq) % OUT_TM == 0
    bf = jnp.bfloat16
    row = lambda a: a.reshape(1, -1).astype(jnp.float32)
    x2 = x.reshape(batch * seq, D_MODEL)
    heads_per_tile = MXU_TILE // HEAD_DIM
    qg = row(jnp.tile(q_norm_g, heads_per_tile)) * (HEAD_DIM ** -0.5 * LOG2_E)
    kg = row(jnp.tile(k_norm_g, heads_per_tile))
    slabs = lambda a: a.astype(jnp.float32).reshape(-1, CONV_SLABS, LANES).transpose(1, 0, 2)
    ya, sgb, q, k, v = _in_proj(x2, row(norm_g), w_in.astype(bf), _head_sum_matrix(), qg, kg,
                                slabs(conv_w), slabs(conv_b), slabs(conv_norm_g), slabs(conv_norm_b),
                                conv_pw_w.astype(bf), row(conv_pw_b), seq)
    score_bound = (SCORE_BOUND_SLACK * HEAD_DIM ** 0.5 * LOG2_E
                   * jnp.max(jnp.abs(q_norm_g)) * jnp.max(jnp.abs(k_norm_g))).astype(jnp.float32)
    bias = _attention_bias()
    yb = lax.cond(
        score_bound <= MAX_SHARED_SHIFT,
        lambda: _attention(q, k, v, sgb, bias - score_bound, batch, seq, row_max=False),
        lambda: _attention(q, k, v, sgb, bias, batch, seq, row_max=True))
    y = _out_proj(ya, yb.reshape(batch * seq, ATTN_WIDTH), x2, w_out.astype(bf))
    return y.reshape(batch, seq, D_MODEL)


def kernel(x, norm_g, w_in, conv_w, conv_b, conv_norm_g, conv_norm_b, conv_pw_w, conv_pw_b,
           q_norm_g, k_norm_g, w_out):
    for i in range(norm_g.shape[0]):
        x = _layer(x, norm_g[i], w_in[i], conv_w[i], conv_b[i], conv_norm_g[i], conv_norm_b[i],
                   conv_pw_w[i], conv_pw_b[i], q_norm_g[i], k_norm_g[i], w_out[i])
    return x
```

```python
import functools

import jax
import jax.numpy as jnp
from jax import lax
from jax.experimental import pallas as pl
from jax.experimental.pallas import tpu as pltpu

D_MODEL = 2048
CONV_WIDTH = 1024
CONV_KERNEL = 31
ATTN_WIDTH = 1024
N_HEADS = 16
HEAD_DIM = 64
IN_WIDTH = 3 * CONV_WIDTH + 4 * ATTN_WIDTH
DILATED_PATTERNS = ((128, 1), (512, 4), (2048, 16))
RMS_EPS = 1e-6
LN_EPS = 1e-5
MASK_VALUE = -1e30
LOG2_E = 1.4426950408889634
SCORE_BOUND_SLACK = 1.02
MAX_SHARED_SHIFT = 40.0

LANES = 128
HEADS_PER_SLAB = LANES // HEAD_DIM
N_SLABS = N_HEADS // HEADS_PER_SLAB
ATTN_BLOCK = 128
MXU_TILE = 256
HALO = 32

IN_TM = 512
OUT_TM = 512
CONV_SLABS = CONV_WIDTH // LANES
CONV_ROWS = 128
NORM_ROWS = 64
MIX_ROWS = 256
VMEM_LIMIT = 60 * 1024 * 1024

_OFF_VAL, _OFF_GLU, _OFF_AGATE = 0, CONV_WIDTH, 2 * CONV_WIDTH
_OFF_Q = 3 * CONV_WIDTH
_OFF_K, _OFF_V, _OFF_BGATE = _OFF_Q + ATTN_WIDTH, _OFF_Q + 2 * ATTN_WIDTH, _OFF_Q + 3 * ATTN_WIDTH


def _sigmoid(z):
    return 1.0 / (1.0 + jnp.exp(-z))


def _in_proj_kernel(tiles_per_seq, x_ref, g_ref, w_ref, hsum_ref, qg_ref, kg_ref,
                    cw_ref, cb_ref, lng_ref, lnb_ref, pww_ref, pwb_ref,
                    ya_ref, sgb_ref, q_ref, k_ref, v_ref,
                    h_scr, u_scr, c_scr, a_scr, sga_scr):
    tm = x_ref.shape[0]
    x = x_ref[...]
    ms = jnp.mean(x * x, axis=-1, keepdims=True)
    h_scr[...] = (x * lax.rsqrt(ms + RMS_EPS) * g_ref[...]).astype(jnp.bfloat16)

    @pl.when(pl.program_id(0) % tiles_per_seq == 0)
    def _():
        u_scr[:, 0:HALO, :] = jnp.zeros((CONV_SLABS, HALO, LANES), jnp.float32)

    def proj(col):
        return jnp.dot(h_scr[...], w_ref[:, col:col + MXU_TILE],
                       preferred_element_type=jnp.float32)

    def conv_chunk(c, row0):
        base = row0 + HALO - (CONV_KERNEL - 1)
        acc = jnp.broadcast_to(cb_ref[c], (CONV_ROWS, LANES))
        for t in range(CONV_KERNEL):
            acc = acc + u_scr[c, base + t:base + t + CONV_ROWS, :] * cw_ref[c, t:t + 1, :]
        c_scr[c, row0:row0 + CONV_ROWS, :] = acc

    def norm_chunk(row0):
        rows = slice(row0, row0 + NORM_ROWS)
        parts = [c_scr[c, rows, :] for c in range(CONV_SLABS)]
        mu = jnp.sum(functools.reduce(jnp.add, parts), axis=-1, keepdims=True) * (1.0 / CONV_WIDTH)
        cens = [p - mu for p in parts]
        var = jnp.sum(functools.reduce(jnp.add, [d * d for d in cens]), axis=-1, keepdims=True) * (1.0 / CONV_WIDTH)
        inv = lax.rsqrt(var + LN_EPS)
        for c in range(CONV_SLABS):
            yn = cens[c] * inv * lng_ref[c] + lnb_ref[c]
            a_scr[rows, c * LANES:(c + 1) * LANES] = (yn * _sigmoid(yn)).astype(jnp.bfloat16)

    slabs_per_tile = MXU_TILE // LANES
    for c in range(CONV_WIDTH // MXU_TILE):
        lo = c * MXU_TILE
        u = proj(_OFF_VAL + lo) * _sigmoid(proj(_OFF_GLU + lo))
        for s in range(slabs_per_tile):
            u_scr[c * slabs_per_tile + s, HALO:HALO + tm, :] = u[:, s * LANES:(s + 1) * LANES]
        z = proj(_OFF_AGATE + lo)
        sga_scr[:, lo:lo + MXU_TILE] = (z * _sigmoid(z)).astype(jnp.bfloat16)

    def head_norm(z, gain):
        ms = jnp.dot((z * z).astype(jnp.bfloat16), hsum_ref[...],
                     preferred_element_type=jnp.float32)
        return z * lax.rsqrt(ms + RMS_EPS) * gain

    def store_slabs(ref, c, val):
        for s in range(slabs_per_tile):
            ref[c * slabs_per_tile + s] = val[:, s * LANES:(s + 1) * LANES].astype(jnp.bfloat16)

    def conv_slabs(first):
        for c in range(first, first + slabs_per_tile):
            for row0 in range(0, tm, CONV_ROWS):
                conv_chunk(c, row0)

    def norm_rows(first, stop):
        for row0 in range(first, stop, NORM_ROWS):
            norm_chunk(row0)

    def pointwise(first):
        for c in range(first, first + 2):
            cols = slice(c * MXU_TILE, (c + 1) * MXU_TILE)
            pw = jnp.dot(a_scr[...], pww_ref[:, cols], preferred_element_type=jnp.float32) + pwb_ref[:, cols]
            ya_ref[:, cols] = (pw * sga_scr[:, cols].astype(jnp.float32)).astype(jnp.bfloat16)

    mixer_a = [functools.partial(conv_slabs, c) for c in range(0, CONV_SLABS, slabs_per_tile)]
    mixer_a += [functools.partial(norm_rows, 0, tm // 2), functools.partial(norm_rows, tm // 2, tm)]
    mixer_a += [functools.partial(pointwise, 0), functools.partial(pointwise, 2)]
    for c in range(ATTN_WIDTH // MXU_TILE):
        lo = c * MXU_TILE
        zq, zk = proj(_OFF_Q + lo), proj(_OFF_K + lo)
        nq, nk = head_norm(zq, qg_ref[...]), head_norm(zk, kg_ref[...])
        store_slabs(q_ref, c, nq)
        store_slabs(k_ref, c, nk)
        mixer_a[2 * c]()
        zv, zb = proj(_OFF_V + lo), proj(_OFF_BGATE + lo)
        store_slabs(v_ref, c, zv)
        sgb_ref[:, lo:lo + MXU_TILE] = (zb * _sigmoid(zb)).astype(jnp.bfloat16)
        mixer_a[2 * c + 1]()

    u_scr[:, 0:HALO, :] = u_scr[:, tm:tm + HALO, :]


def _in_proj(x2, norm_g, w_in, hsum, qg, kg, cw, cb, lng, lnb, pww, pwb, seq):
    n = x2.shape[0]
    const = lambda i: (0, 0)
    const3 = lambda i: (0, 0, 0)
    tok = lambda i: (i, 0)
    slab = lambda i: (0, i, 0)
    bf = jnp.bfloat16
    resident = functools.partial(pl.BlockSpec, index_map=const, pipeline_mode=pl.Buffered(1))
    slab_rows = lambda rows: pl.BlockSpec((CONV_SLABS, rows, LANES), const3)
    return pl.pallas_call(
        functools.partial(_in_proj_kernel, seq // IN_TM),
        grid=(n // IN_TM,),
        in_specs=[
            pl.BlockSpec((IN_TM, D_MODEL), tok),
            pl.BlockSpec((1, D_MODEL), const),
            resident((D_MODEL, IN_WIDTH)),
            pl.BlockSpec((MXU_TILE, MXU_TILE), const),
            pl.BlockSpec((1, MXU_TILE), const),
            pl.BlockSpec((1, MXU_TILE), const),
            slab_rows(CONV_KERNEL),
            slab_rows(1),
            slab_rows(1),
            slab_rows(1),
            resident((CONV_WIDTH, CONV_WIDTH)),
            pl.BlockSpec((1, CONV_WIDTH), const),
        ],
        out_specs=[
            pl.BlockSpec((IN_TM, CONV_WIDTH), tok),
            pl.BlockSpec((IN_TM, ATTN_WIDTH), tok),
            pl.BlockSpec((N_SLABS, IN_TM, LANES), slab),
            pl.BlockSpec((N_SLABS, IN_TM, LANES), slab),
            pl.BlockSpec((N_SLABS, IN_TM, LANES), slab),
        ],
        out_shape=[
            jax.ShapeDtypeStruct((n, CONV_WIDTH), bf),
            jax.ShapeDtypeStruct((n, ATTN_WIDTH), bf),
            jax.ShapeDtypeStruct((N_SLABS, n, LANES), bf),
            jax.ShapeDtypeStruct((N_SLABS, n, LANES), bf),
            jax.ShapeDtypeStruct((N_SLABS, n, LANES), bf),
        ],
        scratch_shapes=[
            pltpu.VMEM((IN_TM, D_MODEL), bf),
            pltpu.VMEM((CONV_SLABS, HALO + IN_TM, LANES), jnp.float32),
            pltpu.VMEM((CONV_SLABS, IN_TM, LANES), jnp.float32),
            pltpu.VMEM((IN_TM, CONV_WIDTH), bf),
            pltpu.VMEM((IN_TM, CONV_WIDTH), bf),
        ],
        compiler_params=pltpu.CompilerParams(
            dimension_semantics=("arbitrary",), vmem_limit_bytes=VMEM_LIMIT),
        name="in_proj",
    )(x2, norm_g, w_in, hsum, qg, kg, cw, cb, lng, lnb, pww, pwb)


def _score_stage(q_blk, k_blk, bias, first, mask_a, mask_b, row_max):
    q2 = jnp.concatenate([q_blk * mask_a, q_blk * mask_b], axis=0)
    s = lax.dot_general(q2, k_blk, (((1,), (1,)), ((), ())),
                        preferred_element_type=jnp.float32)
    s = s + jnp.concatenate([bias, bias], axis=0)
    if not row_max:
        return jnp.exp2(s).astype(jnp.bfloat16), None
    m = jnp.max(s, axis=-1, keepdims=True)
    p = jnp.exp2(s - m).astype(jnp.bfloat16)
    m_b = jnp.broadcast_to(m, (2 * ATTN_BLOCK, LANES))
    return p, jnp.where(first, m_b[:ATTN_BLOCK], m_b[ATTN_BLOCK:])


def _value_stage(p, v_blk, first):
    pv = jnp.dot(p, v_blk, preferred_element_type=jnp.float32)
    l = jnp.sum(p.astype(jnp.float32), axis=-1, keepdims=True)
    l_b = jnp.broadcast_to(l, (2 * ATTN_BLOCK, LANES))
    acc = jnp.where(first, pv[:ATTN_BLOCK], pv[ATTN_BLOCK:])
    den = jnp.where(first, l_b[:ATTN_BLOCK], l_b[ATTN_BLOCK:])
    return acc, den


def _attention_kernel(row_max, q_ref, k_ref, v_ref, sgb_ref, bias_ref, o_ref,
                      f32_scr, a4_scr, q4_scr, k4_scr, v4_scr, q16_scr, k16_scr, v16_scr,
                      acc_scr, den_scr, aux_scr):
    seq = q_ref.shape[0]
    bf = jnp.bfloat16
    top_scr = mod4_scr = aux_scr
    lane = lax.broadcasted_iota(jnp.int32, (ATTN_BLOCK, LANES), 1)
    first = lane < HEAD_DIM
    mask_a = first.astype(jnp.float32).astype(bf)
    mask_b = 1 - mask_a

    for src, dst4, dst16 in ((q_ref, q4_scr, q16_scr), (k_ref, k4_scr, k16_scr), (v_ref, v4_scr, v16_scr)):
        f32_scr[...] = src[...].astype(jnp.float32)
        for r4 in range(4):
            part = f32_scr[pl.ds(r4, seq // 4, stride=4), :]
            a4_scr[r4] = part
            dst4[r4] = part.astype(bf)
        for r4 in range(4):
            for hi in range(4):
                dst16[r4 + 4 * hi] = a4_scr[r4, pl.ds(hi, seq // 16, stride=4), :].astype(bf)

    n_tiles = seq // ATTN_BLOCK

    patterns = (
        (1, lambda r, rows: q_ref[rows, :], lambda r, rows: k_ref[rows, :], lambda r, rows: v_ref[rows, :]),
        (4, lambda r, rows: q4_scr[r, rows, :], lambda r, rows: k4_scr[r, rows, :],
         lambda r, rows: v4_scr[r, rows, :]),
        (16, lambda r, rows: q16_scr[r, rows, :], lambda r, rows: k16_scr[r, rows, :],
         lambda r, rows: v16_scr[r, rows, :]),
    )

    for p_idx, (dil, q_src, k_src, v_src) in enumerate(patterns):
        blk_per_res = n_tiles // dil
        for i in range(n_tiles):
            res, n = divmod(i, blk_per_res)
            row0 = n * ATTN_BLOCK
            k_rows = pl.ds(max(row0 - ATTN_BLOCK, 0), 2 * ATTN_BLOCK)
            p, top = _score_stage(q_src(res, pl.ds(row0, ATTN_BLOCK)), k_src(res, k_rows),
                                  bias_ref[min(n, 1)], first, mask_a, mask_b, row_max)
            acc, den = _value_stage(p, v_src(res, k_rows), first)
            if dil == 16 and not row_max:
                out = pl.ds(4 * row0 + res // 4, ATTN_BLOCK, stride=4)
                mod4_scr[0, res % 4, out, :] = acc
                mod4_scr[1, res % 4, out, :] = den
                continue
            out = pl.ds(dil * row0 + res, ATTN_BLOCK, stride=dil) if dil > 1 else pl.ds(row0, ATTN_BLOCK)
            if row_max:
                top_scr[p_idx, out, :] = top
            acc_scr[p_idx, out, :] = acc
            den_scr[p_idx, out, :] = den

    if not row_max:
        for r4 in range(4):
            rows = pl.ds(r4, seq // 4, stride=4)
            acc_scr[2, rows, :] = mod4_scr[0, r4]
            den_scr[2, rows, :] = mod4_scr[1, r4]

    def mix(i, carry):
        rows = pl.ds(pl.multiple_of(i * MIX_ROWS, MIX_ROWS), MIX_ROWS)
        if row_max:
            tops = [top_scr[p, rows, :] for p in range(3)]
            top = jnp.maximum(jnp.maximum(tops[0], tops[1]), tops[2])
            ws = [jnp.exp2(t - top) for t in tops]
            den = ws[0] * den_scr[0, rows, :] + ws[1] * den_scr[1, rows, :] + ws[2] * den_scr[2, rows, :]
            num = ws[0] * acc_scr[0, rows, :] + ws[1] * acc_scr[1, rows, :] + ws[2] * acc_scr[2, rows, :]
        else:
            den = den_scr[0, rows, :] + den_scr[1, rows, :] + den_scr[2, rows, :]
            num = acc_scr[0, rows, :] + acc_scr[1, rows, :] + acc_scr[2, rows, :]
        mixed = (num / den).astype(bf)
        o_ref[rows, :] = mixed * sgb_ref[rows, :]
        return carry

    lax.fori_loop(0, seq // MIX_ROWS, mix, 0)


def _attention(q, k, v, sgb, bias, batch, seq, row_max):
    bf = jnp.bfloat16
    slab = pl.BlockSpec((None, None, seq, LANES), lambda b, s: (s, b, 0, 0))
    tok = pl.BlockSpec((None, seq, LANES), lambda b, s: (b, 0, s))
    residue_major = lambda d: pltpu.VMEM((d, seq // d, LANES), bf)
    per_pattern = pltpu.VMEM((len(DILATED_PATTERNS), seq, LANES), jnp.float32)
    return pl.pallas_call(
        functools.partial(_attention_kernel, row_max),
        grid=(batch, N_SLABS),
        in_specs=[slab, slab, slab, tok,
                  pl.BlockSpec((2, ATTN_BLOCK, 2 * ATTN_BLOCK), lambda b, s: (0, 0, 0))],
        out_specs=tok,
        out_shape=jax.ShapeDtypeStruct((batch, seq, ATTN_WIDTH), bf),
        scratch_shapes=[
            pltpu.VMEM((seq, LANES), jnp.float32),
            pltpu.VMEM((4, seq // 4, LANES), jnp.float32),
        ] + [residue_major(4)] * 3 + [residue_major(16)] * 3 + [per_pattern] * 2 + [per_pattern if row_max else pltpu.VMEM((2, 4, seq // 4, LANES), jnp.float32)],
        compiler_params=pltpu.CompilerParams(
            dimension_semantics=("arbitrary", "arbitrary"), vmem_limit_bytes=VMEM_LIMIT),
        name="dilated_attention_rowmax" if row_max else "dilated_attention",
    )(q.reshape(N_SLABS, batch, seq, LANES), k.reshape(N_SLABS, batch, seq, LANES),
      v.reshape(N_SLABS, batch, seq, LANES), sgb.reshape(batch, seq, ATTN_WIDTH), bias)


def _out_proj_kernel(ya_ref, yb_ref, x_ref, wout_ref, y_ref):
    y = jnp.dot(ya_ref[...], wout_ref[0:CONV_WIDTH, :], preferred_element_type=jnp.float32)
    y = y + jnp.dot(yb_ref[...], wout_ref[CONV_WIDTH:, :], preferred_element_type=jnp.float32)
    y_ref[...] = x_ref[...] + y


def _out_proj(ya, yb, x2, wout):
    n = x2.shape[0]
    tok = lambda i: (i, 0)
    return pl.pallas_call(
        _out_proj_kernel,
        grid=(n // OUT_TM,),
        in_specs=[
            pl.BlockSpec((OUT_TM, CONV_WIDTH), tok),
            pl.BlockSpec((OUT_TM, ATTN_WIDTH), tok),
            pl.BlockSpec((OUT_TM, D_MODEL), tok),
            pl.BlockSpec((D_MODEL, D_MODEL), lambda i: (0, 0), pipeline_mode=pl.Buffered(1)),
        ],
        out_specs=pl.BlockSpec((OUT_TM, D_MODEL), tok),
        out_shape=jax.ShapeDtypeStruct((n, D_MODEL), jnp.float32),
        compiler_params=pltpu.CompilerParams(
            dimension_semantics=("arbitrary",), vmem_limit_bytes=VMEM_LIMIT),
        name="out_proj",
    )(ya, yb, x2, wout)


def _attention_bias():
    qi = jnp.arange(ATTN_BLOCK)[:, None]
    ki = jnp.arange(2 * ATTN_BLOCK)[None, :]
    delta = ATTN_BLOCK + qi - ki
    later = (delta >= 0) & (delta <= ATTN_BLOCK)
    first = ki <= qi
    return jnp.where(jnp.stack([first, later]), 0.0, MASK_VALUE).astype(jnp.float32)


def _head_sum_matrix():
    head = jnp.arange(MXU_TILE) // HEAD_DIM
    return jnp.where(head[:, None] == head[None, :], 1.0 / HEAD_DIM, 0.0).astype(jnp.bfloat16)


def _layer(x, norm_g, w_in, conv_w, conv_b, conv_norm_g, conv_norm_b,
           conv_pw_w, conv_pw_b, q_norm_g, k_norm_g, w_out):
    batch, seq, _ = x.shape
    assert all(w // d == ATTN_BLOCK for w, d in DILATED_PATTERNS)
    assert seq % (16 * ATTN_BLOCK) == 0 and seq % IN_TM == 0 and (batch * seq) % OUT_TM == 0
    bf = jnp.bfloat16
    row = lambda a: a.reshape(1, -1).astype(jnp.float32)
    x2 = x.reshape(batch * seq, D_MODEL)
    heads_per_tile = MXU_TILE // HEAD_DIM
    qg = row(jnp.tile(q_norm_g, heads_per_tile)) * (HEAD_DIM ** -0.5 * LOG2_E)
    kg = row(jnp.tile(k_norm_g, heads_per_tile))
    slabs = lambda a: a.astype(jnp.float32).reshape(-1, CONV_SLABS, LANES).transpose(1, 0, 2)
    ya, sgb, q, k, v = _in_proj(x2, row(norm_g), w_in.astype(bf), _head_sum_matrix(), qg, kg,
                                slabs(conv_w), slabs(conv_b), slabs(conv_norm_g), slabs(conv_norm_b),
                                conv_pw_w.astype(bf), row(conv_pw_b), seq)
    score_bound = (SCORE_BOUND_SLACK * HEAD_DIM ** 0.5 * LOG2_E
                   * jnp.max(jnp.abs(q_norm_g)) * jnp.max(jnp.abs(k_norm_g))).astype(jnp.float32)
    bias = _attention_bias()
    yb = lax.cond(
        score_bound <= MAX_SHARED_SHIFT,
        lambda: _attention(q, k, v, sgb, bias - score_bound, batch, seq, row_max=False),
        lambda: _attention(q, k, v, sgb, bias, batch, seq, row_max=True))
    y = _out_proj(ya, yb.reshape(batch * seq, ATTN_WIDTH), x2, w_out.astype(bf))
    return y.reshape(batch, seq, D_MODEL)


def kernel(x, norm_g, w_in, conv_w, conv_b, conv_norm_g, conv_norm_b, conv_pw_w, conv_pw_b,
           q_norm_g, k_norm_g, w_out):
    for i in range(norm_g.shape[0]):
        x = _layer(x, norm_g[i], w_in[i], conv_w[i], conv_b[i], conv_norm_g[i], conv_norm_b[i],
                   conv_pw_w[i], conv_pw_b[i], q_norm_g[i], k_norm_g[i], w_out[i])
    return x
```

```python
import functools

import jax
import jax.numpy as jnp
from jax import lax
from jax.experimental import pallas as pl
from jax.experimental.pallas import tpu as pltpu

D_MODEL = 2048
CONV_WIDTH = 1024
CONV_KERNEL = 31
ATTN_WIDTH = 1024
N_HEADS = 16
HEAD_DIM = 64
IN_WIDTH = 3 * CONV_WIDTH + 4 * ATTN_WIDTH
DILATED_PATTERNS = ((128, 1), (512, 4), (2048, 16))
RMS_EPS = 1e-6
LN_EPS = 1e-5
MASK_VALUE = -1e30
LOG2_E = 1.4426950408889634
SCORE_BOUND_SLACK = 1.02
MAX_SHARED_SHIFT = 40.0

LANES = 128
HEADS_PER_SLAB = LANES // HEAD_DIM
N_SLABS = N_HEADS // HEADS_PER_SLAB
ATTN_BLOCK = 128
MXU_TILE = 256
HALO = 32

IN_TM = 512
OUT_TM = 512
CONV_SLABS = CONV_WIDTH // LANES
CONV_ROWS = 128
NORM_ROWS = 64
MIX_ROWS = 256
VMEM_LIMIT = 60 * 1024 * 1024

_OFF_VAL, _OFF_GLU, _OFF_AGATE = 0, CONV_WIDTH, 2 * CONV_WIDTH
_OFF_Q = 3 * CONV_WIDTH
_OFF_K, _OFF_V, _OFF_BGATE = _OFF_Q + ATTN_WIDTH, _OFF_Q + 2 * ATTN_WIDTH, _OFF_Q + 3 * ATTN_WIDTH


def _sigmoid(z):
    return 1.0 / (1.0 + jnp.exp(-z))


def _in_proj_kernel(tiles_per_seq, x_ref, g_ref, w_ref, hsum_ref, qg_ref, kg_ref,
                    cw_ref, cb_ref, lng_ref, lnb_ref, pww_ref, pwb_ref,
                    ya_ref, sgb_ref, q_ref, k_ref, v_ref,
                    h_scr, u_scr, c_scr, a_scr, sga_scr):
    tm = x_ref.shape[0]
    x = x_ref[...]
    ms = jnp.mean(x * x, axis=-1, keepdims=True)
    h_scr[...] = (x * lax.rsqrt(ms + RMS_EPS) * g_ref[...]).astype(jnp.bfloat16)

    @pl.when(pl.program_id(0) % tiles_per_seq == 0)
    def _():
        u_scr[:, 0:HALO, :] = jnp.zeros((CONV_SLABS, HALO, LANES), jnp.float32)

    def proj(col):
        return jnp.dot(h_scr[...], w_ref[:, col:col + MXU_TILE],
                       preferred_element_type=jnp.float32)

    def conv_chunk(c, row0):
        base = row0 + HALO - (CONV_KERNEL - 1)
        acc = jnp.broadcast_to(cb_ref[c], (CONV_ROWS, LANES))
        for t in range(CONV_KERNEL):
            acc = acc + u_scr[c, base + t:base + t + CONV_ROWS, :] * cw_ref[c, t:t + 1, :]
        c_scr[c, row0:row0 + CONV_ROWS, :] = acc

    def norm_chunk(row0):
        rows = slice(row0, row0 + NORM_ROWS)
        parts = [c_scr[c, rows, :] for c in range(CONV_SLABS)]
        mu = jnp.sum(functools.reduce(jnp.add, parts), axis=-1, keepdims=True) * (1.0 / CONV_WIDTH)
        cens = [p - mu for p in parts]
        var = jnp.sum(functools.reduce(jnp.add, [d * d for d in cens]), axis=-1, keepdims=True) * (1.0 / CONV_WIDTH)
        inv = lax.rsqrt(var + LN_EPS)
        for c in range(CONV_SLABS):
            yn = cens[c] * inv * lng_ref[c] + lnb_ref[c]
            a_scr[rows, c * LANES:(c + 1) * LANES] = (yn * _sigmoid(yn)).astype(jnp.bfloat16)

    slabs_per_tile = MXU_TILE // LANES
    for c in range(CONV_WIDTH // MXU_TILE):
        lo = c * MXU_TILE
        u = proj(_OFF_VAL + lo) * _sigmoid(proj(_OFF_GLU + lo))
        for s in range(slabs_per_tile):
            u_scr[c * slabs_per_tile + s, HALO:HALO + tm, :] = u[:, s * LANES:(s + 1) * LANES]
        z = proj(_OFF_AGATE + lo)
        sga_scr[:, lo:lo + MXU_TILE] = (z * _sigmoid(z)).astype(jnp.bfloat16)

    def head_norm(z, gain):
        ms = jnp.dot((z * z).astype(jnp.bfloat16), hsum_ref[...],
                     preferred_element_type=jnp.float32)
        return z * lax.rsqrt(ms + RMS_EPS) * gain

    def store_slabs(ref, c, val):
        for s in range(slabs_per_tile):
            ref[c * slabs_per_tile + s] = val[:, s * LANES:(s + 1) * LANES].astype(jnp.bfloat16)

    def conv_slabs(first):
        for c in range(first, first + slabs_per_tile):
            for row0 in range(0, tm, CONV_ROWS):
                conv_chunk(c, row0)

    def norm_rows(first, stop):
        for row0 in range(first, stop, NORM_ROWS):
            norm_chunk(row0)

    def pointwise(first):
        for c in range(first, first + 2):
            cols = slice(c * MXU_TILE, (c + 1) * MXU_TILE)
            pw = jnp.dot(a_scr[...], pww_ref[:, cols], preferred_element_type=jnp.float32) + pwb_ref[:, cols]
            ya_ref[:, cols] = (pw * sga_scr[:, cols].astype(jnp.float32)).astype(jnp.bfloat16)

    mixer_a = [functools.partial(conv_slabs, c) for c in range(0, CONV_SLABS, slabs_per_tile)]
    mixer_a += [functools.partial(norm_rows, 0, tm // 2), functools.partial(norm_rows, tm // 2, tm)]
    mixer_a += [functools.partial(pointwise, 0), functools.partial(pointwise, 2)]
    for c in range(ATTN_WIDTH // MXU_TILE):
        lo = c * MXU_TILE
        zq, zk = proj(_OFF_Q + lo), proj(_OFF_K + lo)
        nq, nk = head_norm(zq, qg_ref[...]), head_norm(zk, kg_ref[...])
        store_slabs(q_ref, c, nq)
        store_slabs(k_ref, c, nk)
        mixer_a[2 * c]()
        zv, zb = proj(_OFF_V + lo), proj(_OFF_BGATE + lo)
        store_slabs(v_ref, c, zv)
        sgb_ref[:, lo:lo + MXU_TILE] = (zb * _sigmoid(zb)).astype(jnp.bfloat16)
        mixer_a[2 * c + 1]()

    u_scr[:, 0:HALO, :] = u_scr[:, tm:tm + HALO, :]


def _in_proj(x2, norm_g, w_in, hsum, qg, kg, cw, cb, lng, lnb, pww, pwb, seq):
    n = x2.shape[0]
    const = lambda i: (0, 0)
    const3 = lambda i: (0, 0, 0)
    tok = lambda i: (i, 0)
    slab = lambda i: (0, i, 0)
    bf = jnp.bfloat16
    resident = functools.partial(pl.BlockSpec, index_map=const, pipeline_mode=pl.Buffered(1))
    slab_rows = lambda rows: pl.BlockSpec((CONV_SLABS, rows, LANES), const3)
    return pl.pallas_call(
        functools.partial(_in_proj_kernel, seq // IN_TM),
        grid=(n // IN_TM,),
        in_specs=[
            pl.BlockSpec((IN_TM, D_MODEL), tok),
            pl.BlockSpec((1, D_MODEL), const),
            resident((D_MODEL, IN_WIDTH)),
            pl.BlockSpec((MXU_TILE, MXU_TILE), const),
            pl.BlockSpec((1, MXU_TILE), const),
            pl.BlockSpec((1, MXU_TILE), const),
            slab_rows(CONV_KERNEL),
            slab_rows(1),
            slab_rows(1),
            slab_rows(1),
            resident((CONV_WIDTH, CONV_WIDTH)),
            pl.BlockSpec((1, CONV_WIDTH), const),
        ],
        out_specs=[
            pl.BlockSpec((IN_TM, CONV_WIDTH), tok),
            pl.BlockSpec((IN_TM, ATTN_WIDTH), tok),
            pl.BlockSpec((N_SLABS, IN_TM, LANES), slab),
            pl.BlockSpec((N_SLABS, IN_TM, LANES), slab),
            pl.BlockSpec((N_SLABS, IN_TM, LANES), slab),
        ],
        out_shape=[
            jax.ShapeDtypeStruct((n, CONV_WIDTH), bf),
            jax.ShapeDtypeStruct((n, ATTN_WIDTH), bf),
            jax.ShapeDtypeStruct((N_SLABS, n, LANES), bf),
            jax.ShapeDtypeStruct((N_SLABS, n, LANES), bf),
            jax.ShapeDtypeStruct((N_SLABS, n, LANES), bf),
        ],
        scratch_shapes=[
            pltpu.VMEM((IN_TM, D_MODEL), bf),
            pltpu.VMEM((CONV_SLABS, HALO + IN_TM, LANES), jnp.float32),
            pltpu.VMEM((CONV_SLABS, IN_TM, LANES), jnp.float32),
            pltpu.VMEM((IN_TM, CONV_WIDTH), bf),
            pltpu.VMEM((IN_TM, CONV_WIDTH), bf),
        ],
        compiler_params=pltpu.CompilerParams(
            dimension_semantics=("arbitrary",), vmem_limit_bytes=VMEM_LIMIT),
        name="in_proj",
    )(x2, norm_g, w_in, hsum, qg, kg, cw, cb, lng, lnb, pww, pwb)


def _score_stage(q_blk, k_blk, bias, first, mask_a, mask_b, row_max):
    q2 = jnp.concatenate([q_blk * mask_a, q_blk * mask_b], axis=0)
    s = lax.dot_general(q2, k_blk, (((1,), (1,)), ((), ())),
                        preferred_element_type=jnp.float32)
    s = s + jnp.concatenate([bias, bias], axis=0)
    if not row_max:
        return jnp.exp2(s).astype(jnp.bfloat16), None
    m = jnp.max(s, axis=-1, keepdims=True)
    p = jnp.exp2(s - m).astype(jnp.bfloat16)
    m_b = jnp.broadcast_to(m, (2 * ATTN_BLOCK, LANES))
    return p, jnp.where(first, m_b[:ATTN_BLOCK], m_b[ATTN_BLOCK:])


def _value_stage(p, v_blk, first):
    pv = jnp.dot(p, v_blk, preferred_element_type=jnp.float32)
    l = jnp.sum(p.astype(jnp.float32), axis=-1, keepdims=True)
    l_b = jnp.broadcast_to(l, (2 * ATTN_BLOCK, LANES))
    acc = jnp.where(first, pv[:ATTN_BLOCK], pv[ATTN_BLOCK:])
    den = jnp.where(first, l_b[:ATTN_BLOCK], l_b[ATTN_BLOCK:])
    return acc, den


def _attention_kernel(row_max, q_ref, k_ref, v_ref, sgb_ref, bias_ref, o_ref,
                      f32_scr, a4_scr, q4_scr, k4_scr, v4_scr, q16_scr, k16_scr, v16_scr,
                      acc_scr, den_scr, aux_scr):
    seq = q_ref.shape[0]
    bf = jnp.bfloat16
    top_scr = mod4_scr = aux_scr
    lane = lax.broadcasted_iota(jnp.int32, (ATTN_BLOCK, LANES), 1)
    first = lane < HEAD_DIM
    mask_a = first.astype(jnp.float32).astype(bf)
    mask_b = 1 - mask_a

    for src, dst4, dst16 in ((q_ref, q4_scr, q16_scr), (k_ref, k4_scr, k16_scr), (v_ref, v4_scr, v16_scr)):
        f32_scr[...] = src[...].astype(jnp.float32)
        for r4 in range(4):
            part = f32_scr[pl.ds(r4, seq // 4, stride=4), :]
            a4_scr[r4] = part
            dst4[r4] = part.astype(bf)
        for r4 in range(4):
            for hi in range(4):
                dst16[r4 + 4 * hi] = a4_scr[r4, pl.ds(hi, seq // 16, stride=4), :].astype(bf)

    n_tiles = seq // ATTN_BLOCK

    patterns = (
        (1, lambda r, rows: q_ref[rows, :], lambda r, rows: k_ref[rows, :], lambda r, rows: v_ref[rows, :]),
        (4, lambda r, rows: q4_scr[r, rows, :], lambda r, rows: k4_scr[r, rows, :],
         lambda r, rows: v4_scr[r, rows, :]),
        (16, lambda r, rows: q16_scr[r, rows, :], lambda r, rows: k16_scr[r, rows, :],
         lambda r, rows: v16_scr[r, rows, :]),
    )

    for p_idx, (dil, q_src, k_src, v_src) in enumerate(patterns):
        blk_per_res = n_tiles // dil
        for i in range(n_tiles):
            res, n = divmod(i, blk_per_res)
            row0 = n * ATTN_BLOCK
            k_rows = pl.ds(max(row0 - ATTN_BLOCK, 0), 2 * ATTN_BLOCK)
            p, top = _score_stage(q_src(res, pl.ds(row0, ATTN_BLOCK)), k_src(res, k_rows),
                                  bias_ref[min(n, 1)], first, mask_a, mask_b, row_max)
            acc, den = _value_stage(p, v_src(res, k_rows), first)
            if dil > 1 and not row_max:
                if dil == 4:
                    cls, out = res, pl.ds(row0, ATTN_BLOCK)
                else:
                    cls, out = res % 4, pl.ds(4 * row0 + res // 4, ATTN_BLOCK, stride=4)
                mod4_scr[0, p_idx - 1, cls, out, :] = acc
                mod4_scr[1, p_idx - 1, cls, out, :] = den
                continue
            out = pl.ds(dil * row0 + res, ATTN_BLOCK, stride=dil) if dil > 1 else pl.ds(row0, ATTN_BLOCK)
            if row_max:
                top_scr[p_idx, out, :] = top
            acc_scr[p_idx, out, :] = acc
            den_scr[p_idx, out, :] = den

    if not row_max:
        for r4 in range(4):
            rows = pl.ds(r4, seq // 4, stride=4)
            acc_scr[1, rows, :] = mod4_scr[0, 0, r4] + mod4_scr[0, 1, r4]
            den_scr[1, rows, :] = mod4_scr[1, 0, r4] + mod4_scr[1, 1, r4]

    def mix(i, carry):
        rows = pl.ds(pl.multiple_of(i * MIX_ROWS, MIX_ROWS), MIX_ROWS)
        if row_max:
            tops = [top_scr[p, rows, :] for p in range(3)]
            top = jnp.maximum(jnp.maximum(tops[0], tops[1]), tops[2])
            ws = [jnp.exp2(t - top) for t in tops]
            den = ws[0] * den_scr[0, rows, :] + ws[1] * den_scr[1, rows, :] + ws[2] * den_scr[2, rows, :]
            num = ws[0] * acc_scr[0, rows, :] + ws[1] * acc_scr[1, rows, :] + ws[2] * acc_scr[2, rows, :]
        else:
            den = den_scr[0, rows, :] + den_scr[1, rows, :]
            num = acc_scr[0, rows, :] + acc_scr[1, rows, :]
        mixed = (num / den).astype(bf)
        o_ref[rows, :] = mixed * sgb_ref[rows, :]
        return carry

    lax.fori_loop(0, seq // MIX_ROWS, mix, 0)


def _attention(q, k, v, sgb, bias, batch, seq, row_max):
    bf = jnp.bfloat16
    slab = pl.BlockSpec((None, None, seq, LANES), lambda b, s: (s, b, 0, 0))
    tok = pl.BlockSpec((None, seq, LANES), lambda b, s: (b, 0, s))
    residue_major = lambda d: pltpu.VMEM((d, seq // d, LANES), bf)
    per_pattern = pltpu.VMEM((len(DILATED_PATTERNS), seq, LANES), jnp.float32)
    return pl.pallas_call(
        functools.partial(_attention_kernel, row_max),
        grid=(batch, N_SLABS),
        in_specs=[slab, slab, slab, tok,
                  pl.BlockSpec((2, ATTN_BLOCK, 2 * ATTN_BLOCK), lambda b, s: (0, 0, 0))],
        out_specs=tok,
        out_shape=jax.ShapeDtypeStruct((batch, seq, ATTN_WIDTH), bf),
        scratch_shapes=[
            pltpu.VMEM((seq, LANES), jnp.float32),
            pltpu.VMEM((4, seq // 4, LANES), jnp.float32),
        ] + [residue_major(4)] * 3 + [residue_major(16)] * 3 + (
            [per_pattern] * 3 if row_max else
            [pltpu.VMEM((2, seq, LANES), jnp.float32)] * 2 + [pltpu.VMEM((2, 2, 4, seq // 4, LANES), jnp.float32)]),
        compiler_params=pltpu.CompilerParams(
            dimension_semantics=("arbitrary", "arbitrary"), vmem_limit_bytes=VMEM_LIMIT),
        name="dilated_attention_rowmax" if row_max else "dilated_attention",
    )(q.reshape(N_SLABS, batch, seq, LANES), k.reshape(N_SLABS, batch, seq, LANES),
      v.reshape(N_SLABS, batch, seq, LANES), sgb.reshape(batch, seq, ATTN_WIDTH), bias)


def _out_proj_kernel(ya_ref, yb_ref, x_ref, wout_ref, y_ref):
    y = jnp.dot(ya_ref[...], wout_ref[0:CONV_WIDTH, :], preferred_element_type=jnp.float32)
    y = y + jnp.dot(yb_ref[...], wout_ref[CONV_WIDTH:, :], preferred_element_type=jnp.float32)
    y_ref[...] = x_ref[...] + y


def _out_proj(ya, yb, x2, wout):
    n = x2.shape[0]
    tok = lambda i: (i, 0)
    return pl.pallas_call(
        _out_proj_kernel,
        grid=(n // OUT_TM,),
        in_specs=[
            pl.BlockSpec((OUT_TM, CONV_WIDTH), tok),
            pl.BlockSpec((OUT_TM, ATTN_WIDTH), tok),
            pl.BlockSpec((OUT_TM, D_MODEL), tok),
            pl.BlockSpec((D_MODEL, D_MODEL), lambda i: (0, 0), pipeline_mode=pl.Buffered(1)),
        ],
        out_specs=pl.BlockSpec((OUT_TM, D_MODEL), tok),
        out_shape=jax.ShapeDtypeStruct((n, D_MODEL), jnp.float32),
        compiler_params=pltpu.CompilerParams(
            dimension_semantics=("arbitrary",), vmem_limit_bytes=VMEM_LIMIT),
        name="out_proj",
    )(ya, yb, x2, wout)


def _attention_bias():
    qi = jnp.arange(ATTN_BLOCK)[:, None]
    ki = jnp.arange(2 * ATTN_BLOCK)[None, :]
    delta = ATTN_BLOCK + qi - ki
    later = (delta >= 0) & (delta <= ATTN_BLOCK)
    first = ki <= qi
    return jnp.where(jnp.stack([first, later]), 0.0, MASK_VALUE).astype(jnp.float32)


def _head_sum_matrix():
    head = jnp.arange(MXU_TILE) // HEAD_DIM
    return jnp.where(head[:, None] == head[None, :], 1.0 / HEAD_DIM, 0.0).astype(jnp.bfloat16)


def _layer(x, norm_g, w_in, conv_w, conv_b, conv_norm_g, conv_norm_b,
           conv_pw_w, conv_pw_b, q_norm_g, k_norm_g, w_out):
    batch, seq, _ = x.shape
    assert all(w // d == ATTN_BLOCK for w, d in DILATED_PATTERNS)
    assert seq % (16 * ATTN_BLOCK) == 0 and seq % IN_TM == 0 and (batch * seq) % OUT_TM == 0
    bf = jnp.bfloat16
    row = lambda a: a.reshape(1, -1).astype(jnp.float32)
    x2 = x.reshape(batch * seq, D_MODEL)
    heads_per_tile = MXU_TILE // HEAD_DIM
    qg = row(jnp.tile(q_norm_g, heads_per_tile)) * (HEAD_DIM ** -0.5 * LOG2_E)
    kg = row(jnp.tile(k_norm_g, heads_per_tile))
    slabs = lambda a: a.astype(jnp.float32).reshape(-1, CONV_SLABS, LANES).transpose(1, 0, 2)
    ya, sgb, q, k, v = _in_proj(x2, row(norm_g), w_in.astype(bf), _head_sum_matrix(), qg, kg,
                                slabs(conv_w), slabs(conv_b), slabs(conv_norm_g), slabs(conv_norm_b),
                                conv_pw_w.astype(bf), row(conv_pw_b), seq)
    score_bound = (SCORE_BOUND_SLACK * HEAD_DIM ** 0.5 * LOG2_E
                   * jnp.max(jnp.abs(q_norm_g)) * jnp.max(jnp.abs(k_norm_g))).astype(jnp.float32)
    bias = _attention_bias()
    yb = lax.cond(
        score_bound <= MAX_SHARED_SHIFT,
        lambda: _attention(q, k, v, sgb, bias - score_bound, batch, seq, row_max=False),
        lambda: _attention(q, k, v, sgb, bias, batch, seq, row_max=True))
    y = _out_proj(ya, yb.reshape(batch * seq, ATTN_WIDTH), x2, w_out.astype(bf))
    return y.reshape(batch, seq, D_MODEL)


def kernel(x, norm_g, w_in, conv_w, conv_b, conv_norm_g, conv_norm_b, conv_pw_w, conv_pw_b,
           q_norm_g, k_norm_g, w_out):
    for i in range(norm_g.shape[0]):
        x = _layer(x, norm_g[i], w_in[i], conv_w[i], conv_b[i], conv_norm_g[i], conv_norm_b[i],
                   conv_pw_w[i], conv_pw_b[i], q_norm_g[i], k_norm_g[i], w_out[i])
    return x
```

```python
import functools

import jax
import jax.numpy as jnp
from jax import lax
from jax.experimental import pallas as pl
from jax.experimental.pallas import tpu as pltpu

D_MODEL = 2048
CONV_WIDTH = 1024
CONV_KERNEL = 31
ATTN_WIDTH = 1024
N_HEADS = 16
HEAD_DIM = 64
IN_WIDTH = 3 * CONV_WIDTH + 4 * ATTN_WIDTH
DILATED_PATTERNS = ((128, 1), (512, 4), (2048, 16))
RMS_EPS = 1e-6
LN_EPS = 1e-5
MASK_VALUE = -1e30
LOG2_E = 1.4426950408889634
SCORE_BOUND_SLACK = 1.02
MAX_SHARED_SHIFT = 40.0

LANES = 128
HEADS_PER_SLAB = LANES // HEAD_DIM
N_SLABS = N_HEADS // HEADS_PER_SLAB
ATTN_BLOCK = 128
MXU_TILE = 256
HALO = 32

IN_TM = 512
OUT_TM = 512
CONV_SLABS = CONV_WIDTH // LANES
CONV_ROWS = 128
NORM_ROWS = 64
MIX_ROWS = 256
VMEM_LIMIT = 60 * 1024 * 1024

_OFF_VAL, _OFF_GLU, _OFF_AGATE = 0, CONV_WIDTH, 2 * CONV_WIDTH
_OFF_Q = 3 * CONV_WIDTH
_OFF_K, _OFF_V, _OFF_BGATE = _OFF_Q + ATTN_WIDTH, _OFF_Q + 2 * ATTN_WIDTH, _OFF_Q + 3 * ATTN_WIDTH


def _sigmoid(z):
    return 1.0 / (1.0 + jnp.exp(-z))


def _in_proj_kernel(tiles_per_seq, x_ref, g_ref, w_ref, hsum_ref, qg_ref, kg_ref,
                    cw_ref, cb_ref, lng_ref, lnb_ref, pww_ref, pwb_ref,
                    ya_ref, sgb_ref, q_ref, k_ref, v_ref,
                    h_scr, u_scr, c_scr, a_scr, sga_scr):
    tm = x_ref.shape[0]
    x = x_ref[...]
    ms = jnp.mean(x * x, axis=-1, keepdims=True)
    h_scr[...] = (x * lax.rsqrt(ms + RMS_EPS) * g_ref[...]).astype(jnp.bfloat16)

    @pl.when(pl.program_id(0) % tiles_per_seq == 0)
    def _():
        u_scr[:, 0:HALO, :] = jnp.zeros((CONV_SLABS, HALO, LANES), jnp.float32)

    def proj(col):
        return jnp.dot(h_scr[...], w_ref[:, col:col + MXU_TILE],
                       preferred_element_type=jnp.float32)

    def conv_chunk(c, row0):
        base = row0 + HALO - (CONV_KERNEL - 1)
        acc = jnp.broadcast_to(cb_ref[c], (CONV_ROWS, LANES))
        for t in range(CONV_KERNEL):
            acc = acc + u_scr[c, base + t:base + t + CONV_ROWS, :] * cw_ref[c, t:t + 1, :]
        c_scr[c, row0:row0 + CONV_ROWS, :] = acc

    def norm_chunk(row0):
        rows = slice(row0, row0 + NORM_ROWS)
        parts = [c_scr[c, rows, :] for c in range(CONV_SLABS)]
        mu = jnp.sum(functools.reduce(jnp.add, parts), axis=-1, keepdims=True) * (1.0 / CONV_WIDTH)
        cens = [p - mu for p in parts]
        var = jnp.sum(functools.reduce(jnp.add, [d * d for d in cens]), axis=-1, keepdims=True) * (1.0 / CONV_WIDTH)
        inv = lax.rsqrt(var + LN_EPS)
        for c in range(CONV_SLABS):
            yn = cens[c] * inv * lng_ref[c] + lnb_ref[c]
            a_scr[rows, c * LANES:(c + 1) * LANES] = (yn * _sigmoid(yn)).astype(jnp.bfloat16)

    slabs_per_tile = MXU_TILE // LANES
    for c in range(CONV_WIDTH // MXU_TILE):
        lo = c * MXU_TILE
        u = proj(_OFF_VAL + lo) * _sigmoid(proj(_OFF_GLU + lo))
        for s in range(slabs_per_tile):
            u_scr[c * slabs_per_tile + s, HALO:HALO + tm, :] = u[:, s * LANES:(s + 1) * LANES]
        z = proj(_OFF_AGATE + lo)
        sga_scr[:, lo:lo + MXU_TILE] = (z * _sigmoid(z)).astype(jnp.bfloat16)

    def head_norm(z, gain):
        ms = jnp.dot((z * z).astype(jnp.bfloat16), hsum_ref[...],
                     preferred_element_type=jnp.float32)
        return z * lax.rsqrt(ms + RMS_EPS) * gain

    def store_slabs(ref, c, val):
        for s in range(slabs_per_tile):
            ref[c * slabs_per_tile + s] = val[:, s * LANES:(s + 1) * LANES].astype(jnp.bfloat16)

    def conv_slabs(first):
        for c in range(first, first + slabs_per_tile):
            for row0 in range(0, tm, CONV_ROWS):
                conv_chunk(c, row0)

    def norm_rows(first, stop):
        for row0 in range(first, stop, NORM_ROWS):
            norm_chunk(row0)

    def pointwise(first):
        for c in range(first, first + 2):
            cols = slice(c * MXU_TILE, (c + 1) * MXU_TILE)
            pw = jnp.dot(a_scr[...], pww_ref[:, cols], preferred_element_type=jnp.float32) + pwb_ref[:, cols]
            ya_ref[:, cols] = (pw * sga_scr[:, cols].astype(jnp.float32)).astype(jnp.bfloat16)

    mixer_a = [functools.partial(conv_slabs, c) for c in range(0, CONV_SLABS, slabs_per_tile)]
    mixer_a += [functools.partial(norm_rows, 0, tm // 2), functools.partial(norm_rows, tm // 2, tm)]
    mixer_a += [functools.partial(pointwise, 0), functools.partial(pointwise, 2)]
    for c in range(ATTN_WIDTH // MXU_TILE):
        lo = c * MXU_TILE
        zq, zk = proj(_OFF_Q + lo), proj(_OFF_K + lo)
        nq, nk = head_norm(zq, qg_ref[...]), head_norm(zk, kg_ref[...])
        store_slabs(q_ref, c, nq)
        store_slabs(k_ref, c, nk)
        mixer_a[2 * c]()
        zv, zb = proj(_OFF_V + lo), proj(_OFF_BGATE + lo)
        store_slabs(v_ref, c, zv)
        sgb_ref[:, lo:lo + MXU_TILE] = (zb * _sigmoid(zb)).astype(jnp.bfloat16)
        mixer_a[2 * c + 1]()

    u_scr[:, 0:HALO, :] = u_scr[:, tm:tm + HALO, :]


def _in_proj(x2, norm_g, w_in, hsum, qg, kg, cw, cb, lng, lnb, pww, pwb, seq):
    n = x2.shape[0]
    const = lambda i: (0, 0)
    const3 = lambda i: (0, 0, 0)
    tok = lambda i: (i, 0)
    slab = lambda i: (0, i, 0)
    bf = jnp.bfloat16
    resident = functools.partial(pl.BlockSpec, index_map=const, pipeline_mode=pl.Buffered(1))
    slab_rows = lambda rows: pl.BlockSpec((CONV_SLABS, rows, LANES), const3)
    return pl.pallas_call(
        functools.partial(_in_proj_kernel, seq // IN_TM),
        grid=(n // IN_TM,),
        in_specs=[
            pl.BlockSpec((IN_TM, D_MODEL), tok),
            pl.BlockSpec((1, D_MODEL), const),
            resident((D_MODEL, IN_WIDTH)),
            pl.BlockSpec((MXU_TILE, MXU_TILE), const),
            pl.BlockSpec((1, MXU_TILE), const),
            pl.BlockSpec((1, MXU_TILE), const),
            slab_rows(CONV_KERNEL),
            slab_rows(1),
            slab_rows(1),
            slab_rows(1),
            resident((CONV_WIDTH, CONV_WIDTH)),
            pl.BlockSpec((1, CONV_WIDTH), const),
        ],
        out_specs=[
            pl.BlockSpec((IN_TM, CONV_WIDTH), tok),
            pl.BlockSpec((IN_TM, ATTN_WIDTH), tok),
            pl.BlockSpec((N_SLABS, IN_TM, LANES), slab),
            pl.BlockSpec((N_SLABS, IN_TM, LANES), slab),
            pl.BlockSpec((N_SLABS, IN_TM, LANES), slab),
        ],
        out_shape=[
            jax.ShapeDtypeStruct((n, CONV_WIDTH), bf),
            jax.ShapeDtypeStruct((n, ATTN_WIDTH), bf),
            jax.ShapeDtypeStruct((N_SLABS, n, LANES), bf),
            jax.ShapeDtypeStruct((N_SLABS, n, LANES), bf),
            jax.ShapeDtypeStruct((N_SLABS, n, LANES), bf),
        ],
        scratch_shapes=[
            pltpu.VMEM((IN_TM, D_MODEL), bf),
            pltpu.VMEM((CONV_SLABS, HALO + IN_TM, LANES), jnp.float32),
            pltpu.VMEM((CONV_SLABS, IN_TM, LANES), jnp.float32),
            pltpu.VMEM((IN_TM, CONV_WIDTH), bf),
            pltpu.VMEM((IN_TM, CONV_WIDTH), bf),
        ],
        compiler_params=pltpu.CompilerParams(
            dimension_semantics=("arbitrary",), vmem_limit_bytes=VMEM_LIMIT),
        name="in_proj",
    )(x2, norm_g, w_in, hsum, qg, kg, cw, cb, lng, lnb, pww, pwb)


def _score_stage(q_blk, k_blk, bias, first, mask_a, mask_b, row_max):
    q2 = jnp.concatenate([q_blk * mask_a, q_blk * mask_b], axis=0)
    s = lax.dot_general(q2, k_blk, (((1,), (1,)), ((), ())),
                        preferred_element_type=jnp.float32)
    s = s + jnp.concatenate([bias, bias], axis=0)
    if not row_max:
        return jnp.exp2(s).astype(jnp.bfloat16), None
    m = jnp.max(s, axis=-1, keepdims=True)
    p = jnp.exp2(s - m).astype(jnp.bfloat16)
    m_b = jnp.broadcast_to(m, (2 * ATTN_BLOCK, LANES))
    return p, jnp.where(first, m_b[:ATTN_BLOCK], m_b[ATTN_BLOCK:])


def _value_stage(p, v_blk, first):
    pv = jnp.dot(p, v_blk, preferred_element_type=jnp.float32)
    l = jnp.sum(p.astype(jnp.float32), axis=-1, keepdims=True)
    l_b = jnp.broadcast_to(l, (2 * ATTN_BLOCK, LANES))
    acc = jnp.where(first, pv[:ATTN_BLOCK], pv[ATTN_BLOCK:])
    den = jnp.where(first, l_b[:ATTN_BLOCK], l_b[ATTN_BLOCK:])
    return acc, den


def _attention_kernel(row_max, q_ref, k_ref, v_ref, sgb_ref, bias_ref, o_ref,
                      f32_scr, a4_scr, q4_scr, k4_scr, v4_scr, q16_scr, k16_scr, v16_scr,
                      acc_scr, den_scr, aux_scr):
    seq = q_ref.shape[0]
    bf = jnp.bfloat16
    top_scr = mod4_scr = aux_scr
    lane = lax.broadcasted_iota(jnp.int32, (ATTN_BLOCK, LANES), 1)
    first = lane < HEAD_DIM
    mask_a = first.astype(jnp.float32).astype(bf)
    mask_b = 1 - mask_a

    for src, dst4, dst16 in ((q_ref, q4_scr, q16_scr), (k_ref, k4_scr, k16_scr), (v_ref, v4_scr, v16_scr)):
        f32_scr[...] = src[...].astype(jnp.float32)
        for r4 in range(4):
            part = f32_scr[pl.ds(r4, seq // 4, stride=4), :]
            a4_scr[r4] = part
            dst4[r4] = part.astype(bf)
        for r4 in range(4):
            for hi in range(4):
                dst16[r4 + 4 * hi] = a4_scr[r4, pl.ds(hi, seq // 16, stride=4), :].astype(bf)

    n_tiles = seq // ATTN_BLOCK

    patterns = (
        (1, lambda r, rows: q_ref[rows, :], lambda r, rows: k_ref[rows, :], lambda r, rows: v_ref[rows, :]),
        (4, lambda r, rows: q4_scr[r, rows, :], lambda r, rows: k4_scr[r, rows, :],
         lambda r, rows: v4_scr[r, rows, :]),
        (16, lambda r, rows: q16_scr[r, rows, :], lambda r, rows: k16_scr[r, rows, :],
         lambda r, rows: v16_scr[r, rows, :]),
    )

    for p_idx, (dil, q_src, k_src, v_src) in enumerate(patterns):
        blk_per_res = n_tiles // dil
        for i in range(n_tiles):
            res, n = divmod(i, blk_per_res)
            row0 = n * ATTN_BLOCK
            if n == 0:
                k_rows, bias = pl.ds(0, ATTN_BLOCK), bias_ref[0, :, :ATTN_BLOCK]
            else:
                k_rows, bias = pl.ds(row0 - ATTN_BLOCK, 2 * ATTN_BLOCK), bias_ref[1]
            p, top = _score_stage(q_src(res, pl.ds(row0, ATTN_BLOCK)), k_src(res, k_rows),
                                  bias, first, mask_a, mask_b, row_max)
            acc, den = _value_stage(p, v_src(res, k_rows), first)
            if dil > 1 and not row_max:
                if dil == 4:
                    cls, out = res, pl.ds(row0, ATTN_BLOCK)
                else:
                    cls, out = res % 4, pl.ds(4 * row0 + res // 4, ATTN_BLOCK, stride=4)
                mod4_scr[0, p_idx - 1, cls, out, :] = acc
                mod4_scr[1, p_idx - 1, cls, out, :] = den
                continue
            out = pl.ds(dil * row0 + res, ATTN_BLOCK, stride=dil) if dil > 1 else pl.ds(row0, ATTN_BLOCK)
            if row_max:
                top_scr[p_idx, out, :] = top
            acc_scr[p_idx, out, :] = acc
            den_scr[p_idx, out, :] = den

    if not row_max:
        for r4 in range(4):
            rows = pl.ds(r4, seq // 4, stride=4)
            acc_scr[1, rows, :] = mod4_scr[0, 0, r4] + mod4_scr[0, 1, r4]
            den_scr[1, rows, :] = mod4_scr[1, 0, r4] + mod4_scr[1, 1, r4]

    def mix(i, carry):
        rows = pl.ds(pl.multiple_of(i * MIX_ROWS, MIX_ROWS), MIX_ROWS)
        if row_max:
            tops = [top_scr[p, rows, :] for p in range(3)]
            top = jnp.maximum(jnp.maximum(tops[0], tops[1]), tops[2])
            ws = [jnp.exp2(t - top) for t in tops]
            den = ws[0] * den_scr[0, rows, :] + ws[1] * den_scr[1, rows, :] + ws[2] * den_scr[2, rows, :]
            num = ws[0] * acc_scr[0, rows, :] + ws[1] * acc_scr[1, rows, :] + ws[2] * acc_scr[2, rows, :]
        else:
            den = den_scr[0, rows, :] + den_scr[1, rows, :]
            num = acc_scr[0, rows, :] + acc_scr[1, rows, :]
        mixed = (num / den).astype(bf)
        o_ref[rows, :] = mixed * sgb_ref[rows, :]
        return carry

    lax.fori_loop(0, seq // MIX_ROWS, mix, 0)


def _attention(q, k, v, sgb, bias, batch, seq, row_max):
    bf = jnp.bfloat16
    slab = pl.BlockSpec((None, None, seq, LANES), lambda b, s: (s, b, 0, 0))
    tok = pl.BlockSpec((None, seq, LANES), lambda b, s: (b, 0, s))
    residue_major = lambda d: pltpu.VMEM((d, seq // d, LANES), bf)
    per_pattern = pltpu.VMEM((len(DILATED_PATTERNS), seq, LANES), jnp.float32)
    return pl.pallas_call(
        functools.partial(_attention_kernel, row_max),
        grid=(batch, N_SLABS),
        in_specs=[slab, slab, slab, tok,
                  pl.BlockSpec((2, ATTN_BLOCK, 2 * ATTN_BLOCK), lambda b, s: (0, 0, 0))],
        out_specs=tok,
        out_shape=jax.ShapeDtypeStruct((batch, seq, ATTN_WIDTH), bf),
        scratch_shapes=[
            pltpu.VMEM((seq, LANES), jnp.float32),
            pltpu.VMEM((4, seq // 4, LANES), jnp.float32),
        ] + [residue_major(4)] * 3 + [residue_major(16)] * 3 + (
            [per_pattern] * 3 if row_max else
            [pltpu.VMEM((2, seq, LANES), jnp.float32)] * 2 + [pltpu.VMEM((2, 2, 4, seq // 4, LANES), jnp.float32)]),
        compiler_params=pltpu.CompilerParams(
            dimension_semantics=("arbitrary", "arbitrary"), vmem_limit_bytes=VMEM_LIMIT),
        name="dilated_attention_rowmax" if row_max else "dilated_attention",
    )(q.reshape(N_SLABS, batch, seq, LANES), k.reshape(N_SLABS, batch, seq, LANES),
      v.reshape(N_SLABS, batch, seq, LANES), sgb.reshape(batch, seq, ATTN_WIDTH), bias)


def _out_proj_kernel(ya_ref, yb_ref, x_ref, wout_ref, y_ref):
    y = jnp.dot(ya_ref[...], wout_ref[0:CONV_WIDTH, :], preferred_element_type=jnp.float32)
    y = y + jnp.dot(yb_ref[...], wout_ref[CONV_WIDTH:, :], preferred_element_type=jnp.float32)
    y_ref[...] = x_ref[...] + y


def _out_proj(ya, yb, x2, wout):
    n = x2.shape[0]
    tok = lambda i: (i, 0)
    return pl.pallas_call(
        _out_proj_kernel,
        grid=(n // OUT_TM,),
        in_specs=[
            pl.BlockSpec((OUT_TM, CONV_WIDTH), tok),
            pl.BlockSpec((OUT_TM, ATTN_WIDTH), tok),
            pl.BlockSpec((OUT_TM, D_MODEL), tok),
            pl.BlockSpec((D_MODEL, D_MODEL), lambda i: (0, 0), pipeline_mode=pl.Buffered(1)),
        ],
        out_specs=pl.BlockSpec((OUT_TM, D_MODEL), tok),
        out_shape=jax.ShapeDtypeStruct((n, D_MODEL), jnp.float32),
        compiler_params=pltpu.CompilerParams(
            dimension_semantics=("arbitrary",), vmem_limit_bytes=VMEM_LIMIT),
        name="out_proj",
    )(ya, yb, x2, wout)


def _attention_bias():
    qi = jnp.arange(ATTN_BLOCK)[:, None]
    ki = jnp.arange(2 * ATTN_BLOCK)[None, :]
    delta = ATTN_BLOCK + qi - ki
    later = (delta >= 0) & (delta <= ATTN_BLOCK)
    first = ki <= qi
    return jnp.where(jnp.stack([first, later]), 0.0, MASK_VALUE).astype(jnp.float32)


def _head_sum_matrix():
    head = jnp.arange(MXU_TILE) // HEAD_DIM
    return jnp.where(head[:, None] == head[None, :], 1.0 / HEAD_DIM, 0.0).astype(jnp.bfloat16)


def _layer(x, norm_g, w_in, conv_w, conv_b, conv_norm_g, conv_norm_b,
           conv_pw_w, conv_pw_b, q_norm_g, k_norm_g, w_out):
    batch, seq, _ = x.shape
    assert all(w // d == ATTN_BLOCK for w, d in DILATED_PATTERNS)
    assert seq % (16 * ATTN_BLOCK) == 0 and seq % IN_TM == 0 and (batch * seq) % OUT_TM == 0
    bf = jnp.bfloat16
    row = lambda a: a.reshape(1, -1).astype(jnp.float32)
    x2 = x.reshape(batch * seq, D_MODEL)
    heads_per_tile = MXU_TILE // HEAD_DIM
    qg = row(jnp.tile(q_norm_g, heads_per_tile)) * (HEAD_DIM ** -0.5 * LOG2_E)
    kg = row(jnp.tile(k_norm_g, heads_per_tile))
    slabs = lambda a: a.astype(jnp.float32).reshape(-1, CONV_SLABS, LANES).transpose(1, 0, 2)
    ya, sgb, q, k, v = _in_proj(x2, row(norm_g), w_in.astype(bf), _head_sum_matrix(), qg, kg,
                                slabs(conv_w), slabs(conv_b), slabs(conv_norm_g), slabs(conv_norm_b),
                                conv_pw_w.astype(bf), row(conv_pw_b), seq)
    score_bound = (SCORE_BOUND_SLACK * HEAD_DIM ** 0.5 * LOG2_E
                   * jnp.max(jnp.abs(q_norm_g)) * jnp.max(jnp.abs(k_norm_g))).astype(jnp.float32)
    bias = _attention_bias()
    yb = lax.cond(
        score_bound <= MAX_SHARED_SHIFT,
        lambda: _attention(q, k, v, sgb, bias - score_bound, batch, seq, row_max=False),
        lambda: _attention(q, k, v, sgb, bias, batch, seq, row_max=True))
    y = _out_proj(ya, yb.reshape(batch * seq, ATTN_WIDTH), x2, w_out.astype(bf))
    return y.reshape(batch, seq, D_MODEL)


def kernel(x, norm_g, w_in, conv_w, conv_b, conv_norm_g, conv_norm_b, conv_pw_w, conv_pw_b,
           q_norm_g, k_norm_g, w_out):
    for i in range(norm_g.shape[0]):
        x = _layer(x, norm_g[i], w_in[i], conv_w[i], conv_b[i], conv_norm_g[i], conv_norm_b[i],
                   conv_pw_w[i], conv_pw_b[i], q_norm_g[i], k_norm_g[i], w_out[i])
    return x
```

```python
import functools

import jax
import jax.numpy as jnp
from jax import lax
from jax.experimental import pallas as pl
from jax.experimental.pallas import tpu as pltpu

D_MODEL = 2048
CONV_WIDTH = 1024
CONV_KERNEL = 31
ATTN_WIDTH = 1024
N_HEADS = 16
HEAD_DIM = 64
IN_WIDTH = 3 * CONV_WIDTH + 4 * ATTN_WIDTH
DILATED_PATTERNS = ((128, 1), (512, 4), (2048, 16))
RMS_EPS = 1e-6
LN_EPS = 1e-5
MASK_VALUE = -1e30
LOG2_E = 1.4426950408889634
SCORE_BOUND_SLACK = 1.02
MAX_SHARED_SHIFT = 40.0

LANES = 128
HEADS_PER_SLAB = LANES // HEAD_DIM
N_SLABS = N_HEADS // HEADS_PER_SLAB
ATTN_BLOCK = 128
MXU_TILE = 256
HALO = 32

IN_TM = 512
OUT_TM = 512
CONV_SLABS = CONV_WIDTH // LANES
CONV_ROWS = 128
NORM_ROWS = 64
MIX_ROWS = 256
VMEM_LIMIT = 60 * 1024 * 1024

_OFF_VAL, _OFF_GLU, _OFF_AGATE = 0, CONV_WIDTH, 2 * CONV_WIDTH
_OFF_Q = 3 * CONV_WIDTH
_OFF_K, _OFF_V, _OFF_BGATE = _OFF_Q + ATTN_WIDTH, _OFF_Q + 2 * ATTN_WIDTH, _OFF_Q + 3 * ATTN_WIDTH


def _sigmoid(z):
    return 1.0 / (1.0 + jnp.exp(-z))


def _in_proj_kernel(tiles_per_seq, x_ref, g_ref, w_ref, hsum_ref, qg_ref, kg_ref,
                    cw_ref, cb_ref, lng_ref, lnb_ref, pww_ref, pwb_ref,
                    ya_ref, sgb_ref, q_ref, k_ref, v_ref,
                    h_scr, u_scr, c_scr, a_scr, sga_scr):
    tm = x_ref.shape[0]
    x = x_ref[...]
    ms = jnp.mean(x * x, axis=-1, keepdims=True)
    h_scr[...] = (x * lax.rsqrt(ms + RMS_EPS) * g_ref[...]).astype(jnp.bfloat16)

    @pl.when(pl.program_id(0) % tiles_per_seq == 0)
    def _():
        u_scr[:, 0:HALO, :] = jnp.zeros((CONV_SLABS, HALO, LANES), jnp.float32)

    def proj(col):
        return jnp.dot(h_scr[...], w_ref[:, col:col + MXU_TILE],
                       preferred_element_type=jnp.float32)

    def conv_chunk(c, row0):
        base = row0 + HALO - (CONV_KERNEL - 1)
        acc = jnp.broadcast_to(cb_ref[c], (CONV_ROWS, LANES))
        for t in range(CONV_KERNEL):
            acc = acc + u_scr[c, base + t:base + t + CONV_ROWS, :] * cw_ref[c, t:t + 1, :]
        c_scr[c, row0:row0 + CONV_ROWS, :] = acc

    def norm_chunk(row0):
        rows = slice(row0, row0 + NORM_ROWS)
        parts = [c_scr[c, rows, :] for c in range(CONV_SLABS)]
        mu = jnp.sum(functools.reduce(jnp.add, parts), axis=-1, keepdims=True) * (1.0 / CONV_WIDTH)
        cens = [p - mu for p in parts]
        var = jnp.sum(functools.reduce(jnp.add, [d * d for d in cens]), axis=-1, keepdims=True) * (1.0 / CONV_WIDTH)
        inv = lax.rsqrt(var + LN_EPS)
        for c in range(CONV_SLABS):
            yn = cens[c] * inv * lng_ref[c] + lnb_ref[c]
            a_scr[rows, c * LANES:(c + 1) * LANES] = (yn * _sigmoid(yn)).astype(jnp.bfloat16)

    slabs_per_tile = MXU_TILE // LANES
    for c in range(CONV_WIDTH // MXU_TILE):
        lo = c * MXU_TILE
        u = proj(_OFF_VAL + lo) * _sigmoid(proj(_OFF_GLU + lo))
        for s in range(slabs_per_tile):
            u_scr[c * slabs_per_tile + s, HALO:HALO + tm, :] = u[:, s * LANES:(s + 1) * LANES]
        z = proj(_OFF_AGATE + lo)
        sga_scr[:, lo:lo + MXU_TILE] = (z * _sigmoid(z)).astype(jnp.bfloat16)

    def head_norm(z, gain):
        ms = jnp.dot((z * z).astype(jnp.bfloat16), hsum_ref[...],
                     preferred_element_type=jnp.float32)
        return z * lax.rsqrt(ms + RMS_EPS) * gain

    def store_slabs(ref, c, val):
        for s in range(slabs_per_tile):
            ref[c * slabs_per_tile + s] = val[:, s * LANES:(s + 1) * LANES].astype(jnp.bfloat16)

    def conv_slabs(first):
        for c in range(first, first + slabs_per_tile):
            for row0 in range(0, tm, CONV_ROWS):
                conv_chunk(c, row0)

    def norm_rows(first, stop):
        for row0 in range(first, stop, NORM_ROWS):
            norm_chunk(row0)

    def pointwise(first):
        for c in range(first, first + 2):
            cols = slice(c * MXU_TILE, (c + 1) * MXU_TILE)
            pw = jnp.dot(a_scr[...], pww_ref[:, cols], preferred_element_type=jnp.float32) + pwb_ref[:, cols]
            ya_ref[:, cols] = (pw * sga_scr[:, cols].astype(jnp.float32)).astype(jnp.bfloat16)

    mixer_a = [functools.partial(conv_slabs, c) for c in range(0, CONV_SLABS, slabs_per_tile)]
    mixer_a += [functools.partial(norm_rows, 0, tm // 2), functools.partial(norm_rows, tm // 2, tm)]
    mixer_a += [functools.partial(pointwise, 0), functools.partial(pointwise, 2)]
    for c in range(ATTN_WIDTH // MXU_TILE):
        lo = c * MXU_TILE
        zq, zk = proj(_OFF_Q + lo), proj(_OFF_K + lo)
        nq, nk = head_norm(zq, qg_ref[...]), head_norm(zk, kg_ref[...])
        store_slabs(q_ref, c, nq)
        store_slabs(k_ref, c, nk)
        mixer_a[2 * c]()
        zv, zb = proj(_OFF_V + lo), proj(_OFF_BGATE + lo)
        store_slabs(v_ref, c, zv)
        sgb_ref[:, lo:lo + MXU_TILE] = (zb * _sigmoid(zb)).astype(jnp.bfloat16)
        mixer_a[2 * c + 1]()

    u_scr[:, 0:HALO, :] = u_scr[:, tm:tm + HALO, :]


def _in_proj(x2, norm_g, w_in, hsum, qg, kg, cw, cb, lng, lnb, pww, pwb, seq):
    n = x2.shape[0]
    const = lambda i: (0, 0)
    const3 = lambda i: (0, 0, 0)
    tok = lambda i: (i, 0)
    slab = lambda i: (0, i, 0)
    bf = jnp.bfloat16
    resident = functools.partial(pl.BlockSpec, index_map=const, pipeline_mode=pl.Buffered(1))
    slab_rows = lambda rows: pl.BlockSpec((CONV_SLABS, rows, LANES), const3)
    return pl.pallas_call(
        functools.partial(_in_proj_kernel, seq // IN_TM),
        grid=(n // IN_TM,),
        in_specs=[
            pl.BlockSpec((IN_TM, D_MODEL), tok),
            pl.BlockSpec((1, D_MODEL), const),
            resident((D_MODEL, IN_WIDTH)),
            pl.BlockSpec((MXU_TILE, MXU_TILE), const),
            pl.BlockSpec((1, MXU_TILE), const),
            pl.BlockSpec((1, MXU_TILE), const),
            slab_rows(CONV_KERNEL),
            slab_rows(1),
            slab_rows(1),
            slab_rows(1),
            resident((CONV_WIDTH, CONV_WIDTH)),
            pl.BlockSpec((1, CONV_WIDTH), const),
        ],
        out_specs=[
            pl.BlockSpec((IN_TM, CONV_WIDTH), tok),
            pl.BlockSpec((IN_TM, ATTN_WIDTH), tok),
            pl.BlockSpec((N_SLABS, IN_TM, LANES), slab),
            pl.BlockSpec((N_SLABS, IN_TM, LANES), slab),
            pl.BlockSpec((N_SLABS, IN_TM, LANES), slab),
        ],
        out_shape=[
            jax.ShapeDtypeStruct((n, CONV_WIDTH), bf),
            jax.ShapeDtypeStruct((n, ATTN_WIDTH), bf),
            jax.ShapeDtypeStruct((N_SLABS, n, LANES), bf),
            jax.ShapeDtypeStruct((N_SLABS, n, LANES), bf),
            jax.ShapeDtypeStruct((N_SLABS, n, LANES), bf),
        ],
        scratch_shapes=[
            pltpu.VMEM((IN_TM, D_MODEL), bf),
            pltpu.VMEM((CONV_SLABS, HALO + IN_TM, LANES), jnp.float32),
            pltpu.VMEM((CONV_SLABS, IN_TM, LANES), jnp.float32),
            pltpu.VMEM((IN_TM, CONV_WIDTH), bf),
            pltpu.VMEM((IN_TM, CONV_WIDTH), bf),
        ],
        compiler_params=pltpu.CompilerParams(
            dimension_semantics=("arbitrary",), vmem_limit_bytes=VMEM_LIMIT),
        name="in_proj",
    )(x2, norm_g, w_in, hsum, qg, kg, cw, cb, lng, lnb, pww, pwb)


def _score_stage(q_blk, k_blk, bias, first, mask_a, mask_b, row_max):
    q2 = jnp.concatenate([q_blk * mask_a, q_blk * mask_b], axis=0)
    s = lax.dot_general(q2, k_blk, (((1,), (1,)), ((), ())),
                        preferred_element_type=jnp.float32)
    s = s + jnp.concatenate([bias, bias], axis=0)
    if not row_max:
        return jnp.exp2(s).astype(jnp.bfloat16), None
    m = jnp.max(s, axis=-1, keepdims=True)
    p = jnp.exp2(s - m).astype(jnp.bfloat16)
    m_b = jnp.broadcast_to(m, (2 * ATTN_BLOCK, LANES))
    return p, jnp.where(first, m_b[:ATTN_BLOCK], m_b[ATTN_BLOCK:])


def _value_stage(p, v_blk, first):
    pv = jnp.dot(p, v_blk, preferred_element_type=jnp.float32)
    l = jnp.sum(p.astype(jnp.float32), axis=-1, keepdims=True)
    l_b = jnp.broadcast_to(l, (2 * ATTN_BLOCK, LANES))
    acc = jnp.where(first, pv[:ATTN_BLOCK], pv[ATTN_BLOCK:])
    den = jnp.where(first, l_b[:ATTN_BLOCK], l_b[ATTN_BLOCK:])
    return acc, den


def _attention_kernel(row_max, q_ref, k_ref, v_ref, sgb_ref, bias_ref, o_ref,
                      f32_scr, a4_scr, q4_scr, k4_scr, v4_scr, q16_scr, k16_scr, v16_scr,
                      acc_scr, den_scr, aux_scr):
    seq = q_ref.shape[0]
    bf = jnp.bfloat16
    top_scr = mod4_scr = aux_scr
    lane = lax.broadcasted_iota(jnp.int32, (ATTN_BLOCK, LANES), 1)
    first = lane < HEAD_DIM
    mask_a = first.astype(jnp.float32).astype(bf)
    mask_b = 1 - mask_a

    for src, dst4, dst16 in ((q_ref, q4_scr, q16_scr), (k_ref, k4_scr, k16_scr), (v_ref, v4_scr, v16_scr)):
        f32_scr[...] = src[...].astype(jnp.float32)
        for r4 in range(4):
            part = f32_scr[pl.ds(r4, seq // 4, stride=4), :]
            a4_scr[r4] = part
            dst4[r4] = part.astype(bf)
        for r4 in range(4):
            for hi in range(4):
                dst16[r4 + 4 * hi] = a4_scr[r4, pl.ds(hi, seq // 16, stride=4), :].astype(bf)

    n_tiles = seq // ATTN_BLOCK

    patterns = (
        (1, lambda r, rows: q_ref[rows, :], lambda r, rows: k_ref[rows, :], lambda r, rows: v_ref[rows, :]),
        (4, lambda r, rows: q4_scr[r, rows, :], lambda r, rows: k4_scr[r, rows, :],
         lambda r, rows: v4_scr[r, rows, :]),
        (16, lambda r, rows: q16_scr[r, rows, :], lambda r, rows: k16_scr[r, rows, :],
         lambda r, rows: v16_scr[r, rows, :]),
    )

    for p_idx, (dil, q_src, k_src, v_src) in enumerate(patterns):
        blk_per_res = n_tiles // dil
        for i in range(n_tiles):
            res, n = divmod(i, blk_per_res)
            row0 = n * ATTN_BLOCK
            if n == 0:
                k_rows, bias = pl.ds(0, ATTN_BLOCK), bias_ref[0, :, :ATTN_BLOCK]
            else:
                k_rows, bias = pl.ds(row0 - ATTN_BLOCK, 2 * ATTN_BLOCK), bias_ref[1]
            p, top = _score_stage(q_src(res, pl.ds(row0, ATTN_BLOCK)), k_src(res, k_rows),
                                  bias, first, mask_a, mask_b, row_max)
            acc, den = _value_stage(p, v_src(res, k_rows), first)
            if dil > 1 and not row_max:
                if dil == 4:
                    cls, out = res, pl.ds(row0, ATTN_BLOCK)
                else:
                    cls, out = res % 4, pl.ds(4 * row0 + res // 4, ATTN_BLOCK, stride=4)
                mod4_scr[0, p_idx - 1, cls, out, :] = acc
                mod4_scr[1, p_idx - 1, cls, out, :] = den
                continue
            out = pl.ds(dil * row0 + res, ATTN_BLOCK, stride=dil) if dil > 1 else pl.ds(row0, ATTN_BLOCK)
            if row_max:
                top_scr[p_idx, out, :] = top
            acc_scr[p_idx, out, :] = acc
            den_scr[p_idx, out, :] = den

    if not row_max:
        for r4 in range(4):
            rows = pl.ds(r4, seq // 4, stride=4)
            acc_scr[1, rows, :] = mod4_scr[0, 0, r4] + mod4_scr[0, 1, r4]
            den_scr[1, rows, :] = mod4_scr[1, 0, r4] + mod4_scr[1, 1, r4]

    def mix(i, carry):
        rows = pl.ds(pl.multiple_of(i * MIX_ROWS, MIX_ROWS), MIX_ROWS)
        if row_max:
            tops = [top_scr[p, rows, :] for p in range(3)]
            top = jnp.maximum(jnp.maximum(tops[0], tops[1]), tops[2])
            ws = [jnp.exp2(t - top) for t in tops]
            den = ws[0] * den_scr[0, rows, :] + ws[1] * den_scr[1, rows, :] + ws[2] * den_scr[2, rows, :]
            num = ws[0] * acc_scr[0, rows, :] + ws[1] * acc_scr[1, rows, :] + ws[2] * acc_scr[2, rows, :]
        else:
            den = den_scr[0, rows, :] + den_scr[1, rows, :]
            num = acc_scr[0, rows, :] + acc_scr[1, rows, :]
        mixed = (num / den).astype(bf)
        o_ref[rows, :] = mixed * sgb_ref[rows, :]
        return carry

    lax.fori_loop(0, seq // MIX_ROWS, mix, 0)


def _attention(q, k, v, sgb, bias, batch, seq, row_max):
    bf = jnp.bfloat16
    slab = pl.BlockSpec((None, None, seq, LANES), lambda b, s: (s, b, 0, 0))
    tok = pl.BlockSpec((None, seq, LANES), lambda b, s: (b, 0, s))
    residue_major = lambda d: pltpu.VMEM((d, seq // d, LANES), bf)
    per_pattern = pltpu.VMEM((len(DILATED_PATTERNS), seq, LANES), jnp.float32)
    return pl.pallas_call(
        functools.partial(_attention_kernel, row_max),
        grid=(batch, N_SLABS),
        in_specs=[slab, slab, slab, tok,
                  pl.BlockSpec((2, ATTN_BLOCK, 2 * ATTN_BLOCK), lambda b, s: (0, 0, 0))],
        out_specs=tok,
        out_shape=jax.ShapeDtypeStruct((batch, seq, ATTN_WIDTH), bf),
        scratch_shapes=[
            pltpu.VMEM((seq, LANES), jnp.float32),
            pltpu.VMEM((4, seq // 4, LANES), jnp.float32),
        ] + [residue_major(4)] * 3 + [residue_major(16)] * 3 + (
            [per_pattern] * 3 if row_max else
            [pltpu.VMEM((2, seq, LANES), jnp.float32)] * 2 + [pltpu.VMEM((2, 2, 4, seq // 4, LANES), jnp.float32)]),
        compiler_params=pltpu.CompilerParams(
            dimension_semantics=("arbitrary", "arbitrary"), vmem_limit_bytes=VMEM_LIMIT),
        name="dilated_attention_rowmax" if row_max else "dilated_attention",
    )(q.reshape(N_SLABS, batch, seq, LANES), k.reshape(N_SLABS, batch, seq, LANES),
      v.reshape(N_SLABS, batch, seq, LANES), sgb.reshape(batch, seq, ATTN_WIDTH), bias)


def _out_proj_kernel(ya_ref, yb_ref, x_ref, wout_ref, y_ref, w_scr):
    @pl.when(pl.program_id(0) == 0)
    def _():
        for r in range(0, D_MODEL, MXU_TILE):
            w_scr[r:r + MXU_TILE, :] = wout_ref[r:r + MXU_TILE, :].astype(jnp.bfloat16)

    y = jnp.dot(ya_ref[...], w_scr[0:CONV_WIDTH, :], preferred_element_type=jnp.float32)
    y = y + jnp.dot(yb_ref[...], w_scr[CONV_WIDTH:, :], preferred_element_type=jnp.float32)
    y_ref[...] = x_ref[...] + y


def _out_proj(ya, yb, x2, wout):
    n = x2.shape[0]
    tok = lambda i: (i, 0)
    return pl.pallas_call(
        _out_proj_kernel,
        grid=(n // OUT_TM,),
        in_specs=[
            pl.BlockSpec((OUT_TM, CONV_WIDTH), tok),
            pl.BlockSpec((OUT_TM, ATTN_WIDTH), tok),
            pl.BlockSpec((OUT_TM, D_MODEL), tok),
            pl.BlockSpec((D_MODEL, D_MODEL), lambda i: (0, 0), pipeline_mode=pl.Buffered(1)),
        ],
        out_specs=pl.BlockSpec((OUT_TM, D_MODEL), tok),
        out_shape=jax.ShapeDtypeStruct((n, D_MODEL), jnp.float32),
        scratch_shapes=[pltpu.VMEM((D_MODEL, D_MODEL), jnp.bfloat16)],
        compiler_params=pltpu.CompilerParams(
            dimension_semantics=("arbitrary",), vmem_limit_bytes=VMEM_LIMIT),
        name="out_proj",
    )(ya, yb, x2, wout)


def _attention_bias():
    qi = jnp.arange(ATTN_BLOCK)[:, None]
    ki = jnp.arange(2 * ATTN_BLOCK)[None, :]
    delta = ATTN_BLOCK + qi - ki
    later = (delta >= 0) & (delta <= ATTN_BLOCK)
    first = ki <= qi
    return jnp.where(jnp.stack([first, later]), 0.0, MASK_VALUE).astype(jnp.float32)


def _head_sum_matrix():
    head = jnp.arange(MXU_TILE) // HEAD_DIM
    return jnp.where(head[:, None] == head[None, :], 1.0 / HEAD_DIM, 0.0).astype(jnp.bfloat16)


def _layer(x, norm_g, w_in, conv_w, conv_b, conv_norm_g, conv_norm_b,
           conv_pw_w, conv_pw_b, q_norm_g, k_norm_g, w_out):
    batch, seq, _ = x.shape
    assert all(w // d == ATTN_BLOCK for w, d in DILATED_PATTERNS)
    assert seq % (16 * ATTN_BLOCK) == 0 and seq % IN_TM == 0 and (batch * seq) % OUT_TM == 0
    bf = jnp.bfloat16
    row = lambda a: a.reshape(1, -1).astype(jnp.float32)
    x2 = x.reshape(batch * seq, D_MODEL)
    heads_per_tile = MXU_TILE // HEAD_DIM
    qg = row(jnp.tile(q_norm_g, heads_per_tile)) * (HEAD_DIM ** -0.5 * LOG2_E)
    kg = row(jnp.tile(k_norm_g, heads_per_tile))
    slabs = lambda a: a.astype(jnp.float32).reshape(-1, CONV_SLABS, LANES).transpose(1, 0, 2)
    ya, sgb, q, k, v = _in_proj(x2, row(norm_g), w_in.astype(bf), _head_sum_matrix(), qg, kg,
                                slabs(conv_w), slabs(conv_b), slabs(conv_norm_g), slabs(conv_norm_b),
                                conv_pw_w.astype(bf), row(conv_pw_b), seq)
    score_bound = (SCORE_BOUND_SLACK * HEAD_DIM ** 0.5 * LOG2_E
                   * jnp.max(jnp.abs(q_norm_g)) * jnp.max(jnp.abs(k_norm_g))).astype(jnp.float32)
    bias = _attention_bias()
    yb = lax.cond(
        score_bound <= MAX_SHARED_SHIFT,
        lambda: _attention(q, k, v, sgb, bias - score_bound, batch, seq, row_max=False),
        lambda: _attention(q, k, v, sgb, bias, batch, seq, row_max=True))
    y = _out_proj(ya, yb.reshape(batch * seq, ATTN_WIDTH), x2, w_out.astype(jnp.float32))
    return y.reshape(batch, seq, D_MODEL)


def kernel(x, norm_g, w_in, conv_w, conv_b, conv_norm_g, conv_norm_b, conv_pw_w, conv_pw_b,
           q_norm_g, k_norm_g, w_out):
    for i in range(norm_g.shape[0]):
        x = _layer(x, norm_g[i], w_in[i], conv_w[i], conv_b[i], conv_norm_g[i], conv_norm_b[i],
                   conv_pw_w[i], conv_pw_b[i], q_norm_g[i], k_norm_g[i], w_out[i])
    return x
```
